```python
import jax, jax.numpy as jnp
from jax import lax
import numpy as np

D_MODEL = 1024
BATCH = 2
SEQ = 16384
DEPTH = 2
DEC_BATCH = 8
DEC_SEQ = 8192
PAST_LEN = 128

HEAD_DIM = 64
ROT_DIM = HEAD_DIM // 4
ROPE_THETA = 500000.0
DIL_PAIRS = ((128, 1), (512, 4), (2048, 16))
N_GROUPS_A = 3
HEADS_PER_GROUP_A = 8
WIDTH_A = HEADS_PER_GROUP_A * HEAD_DIM
N_HEADS_B = 8
N_KV_HEADS_B = 2
WIDTH_B = N_HEADS_B * HEAD_DIM
KV_WIDTH_B = N_KV_HEADS_B * HEAD_DIM
WINDOW_B = 128
PLE_DIM = 256
A_QKV_COLS = 3 * N_GROUPS_A * WIDTH_A
IN_COLS = A_QKV_COLS + WIDTH_A + WIDTH_B + 2 * KV_WIDTH_B + WIDTH_B + 2 * D_MODEL
EPS = 1e-6
NEG = -1e30

kernel_name = "hybrid_dilated_swa_gqa_encoder"


def rms_norm(x, gain):
    xf = x.astype(jnp.float32)
    y = xf * lax.rsqrt(jnp.mean(xf * xf, axis=-1, keepdims=True) + EPS)
    return (y * gain.astype(jnp.float32)).astype(x.dtype)


def rope_tables(s, dtype):
    inv = ROPE_THETA ** (-jnp.arange(0, ROT_DIM, 2, dtype=jnp.float32) / ROT_DIM)
    ang = jnp.arange(s, dtype=jnp.float32)[:, None] * inv[None, :]
    ang = jnp.concatenate([ang, ang], axis=-1)
    return (jnp.cos(ang)[None, :, None, :].astype(dtype),
            jnp.sin(ang)[None, :, None, :].astype(dtype))


def apply_rope(t, cos, sin):
    tr, tp = t[..., :ROT_DIM], t[..., ROT_DIM:]
    x1, x2 = tr[..., :ROT_DIM // 2], tr[..., ROT_DIM // 2:]
    tr = tr * cos + jnp.concatenate([-x2, x1], axis=-1) * sin
    return jnp.concatenate([tr, tp], axis=-1)


def banded_attention(q, k, v, half_window, sink=None):
    blk = half_window
    b, L, hq, d = q.shape
    hkv = k.shape[2]
    g = hq // hkv
    n_blk = -(-L // blk)
    Lp = n_blk * blk
    pad = Lp - L
    qb = jnp.pad(q, ((0, 0), (0, pad), (0, 0), (0, 0))).reshape(b, n_blk, blk, hkv, g, d)
    kb = jnp.pad(k, ((0, 0), (blk, blk + pad), (0, 0), (0, 0))).reshape(b, n_blk + 2, blk, hkv, d)
    vb = jnp.pad(v, ((0, 0), (blk, blk + pad), (0, 0), (0, 0))).reshape(b, n_blk + 2, blk, hkv, d)
    kw = jnp.concatenate([kb[:, :-2], kb[:, 1:-1], kb[:, 2:]], axis=2)
    vw = jnp.concatenate([vb[:, :-2], vb[:, 1:-1], vb[:, 2:]], axis=2)
    q_pos = jnp.arange(n_blk)[:, None] * blk + jnp.arange(blk)[None, :]
    k_pos = jnp.arange(n_blk)[:, None] * blk - blk + jnp.arange(3 * blk)[None, :]
    mask = ((jnp.abs(q_pos[:, :, None] - k_pos[:, None, :]) <= half_window)
            & (k_pos[:, None, :] >= 0) & (k_pos[:, None, :] < L))
    s = jnp.einsum('bnqhgd,bnkhd->bnhgqk', qb, kw).astype(jnp.float32) * (HEAD_DIM ** -0.5)
    s = jnp.where(mask[None, :, None, None], s, NEG)
    m = jnp.max(s, axis=-1)
    if sink is not None:
        sk = sink.astype(jnp.float32).reshape(1, 1, hkv, g, 1)
        m = jnp.maximum(m, sk)
        e = jnp.exp(s - m[..., None])
        denom = jnp.sum(e, axis=-1) + jnp.exp(sk - m)
    else:
        e = jnp.exp(s - m[..., None])
        denom = jnp.sum(e, axis=-1)
    pr = e / denom[..., None]
    lse = m + jnp.log(denom)
    out = jnp.einsum('bnhgqk,bnkhd->bnqhgd', pr.astype(v.dtype), vw)
    out = out.reshape(b, Lp, hq, d)[:, :L]
    lse = lse.transpose(0, 1, 4, 2, 3).reshape(b, Lp, hq)[:, :L]
    return out, lse


def dilated_group(q, k, v, window, dilation):
    b, s, h, d = q.shape
    L = s // dilation
    half_keys = (window // 2) // dilation

    def to_sub(t):
        return t.reshape(b, L, dilation, h, d).transpose(0, 2, 1, 3, 4).reshape(b * dilation, L, h, d)

    o, lse = banded_attention(to_sub(q), to_sub(k), to_sub(v), half_keys)
    o = o.reshape(b, dilation, L, h, d).transpose(0, 2, 1, 3, 4).reshape(b, s, h, d)
    lse = lse.reshape(b, dilation, L, h).transpose(0, 2, 1, 3).reshape(b, s, h)
    return o, lse


def encoder_layer(x, p, norm_mix, w_in, a_q_norm, a_k_norm, b_q_norm, b_k_norm, b_sink,
                  w_branch_a, w_branch_b, w_out, norm_ple, w_ple, w_ple_gate):
    b, s, _ = x.shape
    hn = rms_norm(x, norm_mix)
    proj = hn @ w_in
    sizes = [A_QKV_COLS, WIDTH_A, WIDTH_B, KV_WIDTH_B, KV_WIDTH_B, WIDTH_B, D_MODEL, D_MODEL]
    offs = [int(o) for o in np.cumsum(sizes)[:-1]]
    a_qkv, a_gate, bq, bk, bv, b_gate, mg_a, mg_b = jnp.split(proj, offs, axis=-1)
    cos, sin = rope_tables(s, x.dtype)

    a_qkv = a_qkv.reshape(b, s, 3, N_GROUPS_A, HEADS_PER_GROUP_A, HEAD_DIM)
    outs, lses = [], []
    for gi, (window, dil) in enumerate(DIL_PAIRS):
        q = apply_rope(rms_norm(a_qkv[:, :, 0, gi], a_q_norm[gi]), cos, sin)
        k = apply_rope(rms_norm(a_qkv[:, :, 1, gi], a_k_norm[gi]), cos, sin)
        o, lse = dilated_group(q, k, a_qkv[:, :, 2, gi], window, dil)
        outs.append(o)
        lses.append(lse)
    wts = jax.nn.softmax(jnp.stack(lses, axis=0), axis=0)
    a_out = jnp.einsum('gbsh,gbshd->bshd', wts.astype(x.dtype), jnp.stack(outs, axis=0))
    a_y = (a_out.reshape(b, s, WIDTH_A) * jax.nn.silu(a_gate)) @ w_branch_a

    q = apply_rope(rms_norm(bq.reshape(b, s, N_HEADS_B, HEAD_DIM), b_q_norm), cos, sin)
    k = apply_rope(rms_norm(bk.reshape(b, s, N_KV_HEADS_B, HEAD_DIM), b_k_norm), cos, sin)
    v = bv.reshape(b, s, N_KV_HEADS_B, HEAD_DIM)
    b_out, _ = banded_attention(q, k, v, WINDOW_B, sink=b_sink)
    b_y = (b_out.reshape(b, s, WIDTH_B) * jax.nn.silu(b_gate)) @ w_branch_b

    merged = jax.nn.sigmoid(mg_a) * a_y + jax.nn.sigmoid(mg_b) * b_y
    x = x + merged @ w_out

    gate = jax.nn.sigmoid(rms_norm(x, norm_ple) @ w_ple_gate)
    return x + gate * (p @ w_ple)


def run_trunk(x, p, norm_mix, w_in, a_q_norm, a_k_norm, b_q_norm, b_k_norm, b_sink,
              w_branch_a, w_branch_b, w_out, norm_ple, w_ple, w_ple_gate):
    for i in range(DEPTH):
        x = encoder_layer(x, p[i], norm_mix[i], w_in[i], a_q_norm[i], a_k_norm[i], b_q_norm[i],
                          b_k_norm[i], b_sink[i], w_branch_a[i], w_branch_b[i], w_out[i],
                          norm_ple[i], w_ple[i], w_ple_gate[i])
    return x


def setup_inputs(seed: int = 0) -> dict:
    key = jax.random.key(seed)
    ks = jax.random.split(key, 17)
    f32 = jnp.float32

    def w(k, shape, fan_in):
        return jax.random.normal(k, shape, f32) * (fan_in ** -0.5)

    def gain(k, shape):
        return 1.0 + 0.1 * jax.random.normal(k, shape, f32)

    return {
        "x_prompt": jax.random.normal(ks[0], (BATCH, SEQ, D_MODEL), f32),
        "x_sample": jax.random.normal(ks[1], (DEC_BATCH, DEC_SEQ, D_MODEL), f32),
        "p_prompt": jax.random.normal(ks[2], (DEPTH, BATCH, SEQ, PLE_DIM), f32),
        "p_sample": jax.random.normal(ks[3], (DEPTH, DEC_BATCH, DEC_SEQ, PLE_DIM), f32),
        "norm_mix": gain(ks[4], (DEPTH, D_MODEL)),
        "w_in": w(ks[5], (DEPTH, D_MODEL, IN_COLS), D_MODEL),
        "a_q_norm": gain(ks[6], (DEPTH, N_GROUPS_A, HEAD_DIM)),
        "a_k_norm": gain(ks[7], (DEPTH, N_GROUPS_A, HEAD_DIM)),
        "b_q_norm": gain(ks[8], (DEPTH, HEAD_DIM)),
        "b_k_norm": gain(ks[9], (DEPTH, HEAD_DIM)),
        "b_sink": 0.5 * jax.random.normal(ks[10], (DEPTH, N_HEADS_B), f32),
        "w_branch_a": w(ks[11], (DEPTH, WIDTH_A, D_MODEL), WIDTH_A),
        "w_branch_b": w(ks[12], (DEPTH, WIDTH_B, D_MODEL), WIDTH_B),
        "w_out": w(ks[13], (DEPTH, D_MODEL, D_MODEL), D_MODEL),
        "norm_ple": gain(ks[14], (DEPTH, D_MODEL)),
        "w_ple": w(ks[15], (DEPTH, PLE_DIM, D_MODEL), PLE_DIM),
        "w_ple_gate": w(ks[16], (DEPTH, D_MODEL, D_MODEL), D_MODEL),
    }


def reference(x_prompt, x_sample, p_prompt, p_sample, norm_mix, w_in, a_q_norm, a_k_norm,
              b_q_norm, b_k_norm, b_sink, w_branch_a, w_branch_b, w_out, norm_ple, w_ple,
              w_ple_gate):
    y_prompt = run_trunk(x_prompt, p_prompt, norm_mix, w_in, a_q_norm, a_k_norm, b_q_norm,
                         b_k_norm, b_sink, w_branch_a, w_branch_b, w_out, norm_ple, w_ple,
                         w_ple_gate)
    y_sample = run_trunk(x_sample, p_sample, norm_mix, w_in, a_q_norm, a_k_norm, b_q_norm,
                         b_k_norm, b_sink, w_branch_a, w_branch_b, w_out, norm_ple, w_ple,
                         w_ple_gate)
    return (y_prompt, y_sample)
```

```python
import functools

import jax
import jax.numpy as jnp
from jax import lax
from jax.experimental import pallas as pl
from jax.experimental.pallas import tpu as pltpu

D_MODEL = 1024
HEAD_DIM = 64
ROT_DIM = HEAD_DIM // 4
ROPE_THETA = 500000.0
DIL_PAIRS = ((128, 1), (512, 4), (2048, 16))
N_GROUPS_A = 3
WIDTH_A = 512
WIDTH_B = 512
KV_WIDTH_B = 128
N_HEADS_B = 8
WINDOW_B = 128
PLE_DIM = 256
A_QKV_COLS = 3 * N_GROUPS_A * WIDTH_A
IN_COLS = A_QKV_COLS + WIDTH_A + WIDTH_B + 2 * KV_WIDTH_B + WIDTH_B + 2 * D_MODEL
EPS = 1e-6
NEG = -1e30

OFF_A_GATE = A_QKV_COLS
OFF_BQ = OFF_A_GATE + WIDTH_A
OFF_BK = OFF_BQ + WIDTH_B
OFF_BV = OFF_BK + KV_WIDTH_B
OFF_B_GATE = OFF_BV + KV_WIDTH_B
OFF_MG_A = OFF_B_GATE + WIDTH_B
OFF_MG_B = OFF_MG_A + D_MODEL

LANES = 128
SLAB = 256
HEADS_PER_SLAB = SLAB // HEAD_DIM
Q_BLOCK = 256
VMEM_LIMIT = 56 * 1024 * 1024

BF16 = jnp.bfloat16
F32 = jnp.float32


def _proj_kernel(x_ref, nm_ref, w_ref, ga_ref, gb_ref, c_ref, s1_ref, s2_ref, ones_ref,
                 qa_ref, qb_ref, g_ref):
    xf = x_ref[...]
    ms = jnp.mean(xf * xf, axis=-1, keepdims=True)
    hn = (xf * lax.rsqrt(ms + EPS) * nm_ref[...]).astype(BF16)

    c1 = c_ref[...]
    s1 = s1_ref[...]
    s2 = s2_ref[...]
    c2 = jnp.concatenate([c1, c1], axis=1)
    s12 = jnp.concatenate([s1, s1], axis=1)
    s22 = jnp.concatenate([s2, s2], axis=1)
    ones = ones_ref[...]

    def proj(lo, width):
        return jnp.dot(hn, w_ref[:, lo:lo + width], preferred_element_type=F32)

    def norm_rope(t, gain, cc, sa, sb, ones_m):
        width = t.shape[1]
        ss = jnp.dot((t * t).astype(BF16), ones_m, preferred_element_type=F32)
        n = t * lax.rsqrt(ss * (1.0 / HEAD_DIM) + EPS) * gain
        up = pltpu.roll(n, width - ROT_DIM // 2, 1)
        dn = pltpu.roll(n, ROT_DIM // 2, 1)
        return n * cc + up * sa + dn * sb

    for j in range(2 * N_GROUPS_A * WIDTH_A // SLAB):
        lo = j * SLAB
        t = proj(lo, SLAB)
        qa_ref[:, lo:lo + SLAB] = norm_rope(t, ga_ref[:, lo:lo + SLAB], c2, s12, s22, ones).astype(BF16)
    for j in range(N_GROUPS_A * WIDTH_A // 512):
        lo = 2 * N_GROUPS_A * WIDTH_A + j * 512
        qa_ref[:, lo:lo + 512] = proj(lo, 512).astype(BF16)
    for j in range(WIDTH_B // SLAB):
        lo = j * SLAB
        t = proj(OFF_BQ + lo, SLAB)
        qb_ref[:, lo:lo + SLAB] = norm_rope(t, gb_ref[:, lo:lo + SLAB], c2, s12, s22, ones).astype(BF16)
    t = proj(OFF_BK, 2 * KV_WIDTH_B)
    kb = norm_rope(t[:, :KV_WIDTH_B], gb_ref[:, WIDTH_B:WIDTH_B + KV_WIDTH_B], c1, s1, s2,
                   ones[:KV_WIDTH_B, :KV_WIDTH_B])
    qb_ref[:, WIDTH_B:WIDTH_B + KV_WIDTH_B] = kb.astype(BF16)
    qb_ref[:, WIDTH_B + KV_WIDTH_B:] = t[:, KV_WIDTH_B:].astype(BF16)
    g_ref[:, 0:512] = proj(OFF_A_GATE, 512).astype(BF16)
    g_ref[:, 512:1024] = proj(OFF_B_GATE, 512).astype(BF16)
    for j in range(2 * D_MODEL // 512):
        g_ref[:, 1024 + j * 512:1536 + j * 512] = proj(OFF_MG_A + j * 512, 512).astype(BF16)


def _project(x2, seq, nm, w, ga, gb, c_t, s1_t, s2_t, ones, tile):
    n = x2.shape[0]
    n_seq_tiles = seq // tile
    const = lambda i: (0, 0)
    tab = lambda i: (i % n_seq_tiles, 0)
    row = lambda i: (i, 0)
    return pl.pallas_call(
        _proj_kernel,
        grid=(n // tile,),
        in_specs=[
            pl.BlockSpec((tile, D_MODEL), row),
            pl.BlockSpec((1, D_MODEL), const),
            pl.BlockSpec((D_MODEL, IN_COLS), const, pipeline_mode=pl.Buffered(1)),
            pl.BlockSpec((1, 2 * N_GROUPS_A * WIDTH_A), const),
            pl.BlockSpec((1, WIDTH_B + KV_WIDTH_B), const),
            pl.BlockSpec((tile, LANES), tab),
            pl.BlockSpec((tile, LANES), tab),
            pl.BlockSpec((tile, LANES), tab),
            pl.BlockSpec((SLAB, SLAB), const),
        ],
        out_specs=[
            pl.BlockSpec((tile, A_QKV_COLS), row),
            pl.BlockSpec((tile, WIDTH_B + 2 * KV_WIDTH_B), row),
            pl.BlockSpec((tile, 3 * D_MODEL), row),
        ],
        out_shape=[
            jax.ShapeDtypeStruct((n, A_QKV_COLS), BF16),
            jax.ShapeDtypeStruct((n, WIDTH_B + 2 * KV_WIDTH_B), BF16),
            jax.ShapeDtypeStruct((n, 3 * D_MODEL), BF16),
        ],
        compiler_params=pltpu.CompilerParams(
            dimension_semantics=("arbitrary",), vmem_limit_bytes=VMEM_LIMIT),
        name="proj_in",
    )(x2, nm, w, ga, gb, c_t, s1_t, s2_t, ones)


def _band_bias(base, half_window, n_keys, length):
    qi = lax.broadcasted_iota(jnp.int32, (Q_BLOCK, 1), 0)
    jj = lax.broadcasted_iota(jnp.int32, (Q_BLOCK, n_keys), 1)
    lo = jnp.maximum(qi, half_window - base)
    hi = jnp.minimum(qi + 2 * half_window, length - 1 - base + half_window)
    return jnp.where(jj >= lo, jnp.where(jj <= hi, 0.0, NEG), NEG).astype(F32)


def _slab_attention(q_slab, k_ops, v_ops, bias, sinks):
    head_of_lane = lax.broadcasted_iota(jnp.int32, (1, SLAB), 1) // HEAD_DIM
    zero = jnp.zeros((), BF16)
    acc = None
    stats = []
    for h in range(HEADS_PER_SLAB):
        in_head = head_of_lane == h
        qm = jnp.where(in_head, q_slab, zero)
        s = lax.dot_general(qm, k_ops[h], (((1,), (1,)), ((), ())),
                            preferred_element_type=F32) + bias
        m = jnp.max(s, axis=-1, keepdims=True)
        if sinks is not None:
            m = jnp.maximum(m, sinks[h])
        e = jnp.exp(s - m)
        l = jnp.sum(e, axis=-1, keepdims=True)
        if sinks is not None:
            l = l + jnp.exp(sinks[h] - m)
        vm = jnp.where(in_head, v_ops[h], zero)
        pv = jnp.dot(e.astype(BF16), vm, preferred_element_type=F32)
        acc = pv if acc is None else acc + pv
        stats.append((m, l))
    inv = 1.0 / stats[HEADS_PER_SLAB - 1][1]
    for h in range(HEADS_PER_SLAB - 2, -1, -1):
        inv = jnp.where(head_of_lane == h, 1.0 / stats[h][1], inv)
    return acc * inv, stats


def _attn_a_kernel(q_ref, km_ref, kp_ref, kn_ref, vm_ref, vp_ref, vn_ref, o_ref, lse_ref,
                   kfull, vfull, *, half_window, tile, length):
    w = half_window
    n_keys = Q_BLOCK + 2 * w
    kfull[0:w] = kp_ref[...]
    kfull[w:w + tile] = km_ref[...]
    kfull[w + tile:] = kn_ref[...]
    vfull[0:w] = vp_ref[...]
    vfull[w:w + tile] = vm_ref[...]
    vfull[w + tile:] = vn_ref[...]
    t0 = pl.program_id(2) * tile
    lane = lax.broadcasted_iota(jnp.int32, (1, LANES), 1)
    for i in range(tile // Q_BLOCK):
        r0 = i * Q_BLOCK
        bias = _band_bias(t0 + r0, w, n_keys, length)
        lse_rows = jnp.zeros((Q_BLOCK, LANES), F32)
        for c in range(WIDTH_A // SLAB):
            cols = slice(c * SLAB, (c + 1) * SLAB)
            kw = kfull[r0:r0 + n_keys, cols]
            vw = vfull[r0:r0 + n_keys, cols]
            out, stats = _slab_attention(q_ref[r0:r0 + Q_BLOCK, cols], [kw] * HEADS_PER_SLAB,
                                         [vw] * HEADS_PER_SLAB, bias, None)
            o_ref[r0:r0 + Q_BLOCK, cols] = out.astype(BF16)
            for h in range(HEADS_PER_SLAB):
                m, l = stats[h]
                lse_rows = jnp.where(lane == c * HEADS_PER_SLAB + h, m + jnp.log(l), lse_rows)
        lse_ref[r0:r0 + Q_BLOCK, :] = lse_rows


def _attention_a(qa3, group, dilation, window, tile):
    b, s, _ = qa3.shape
    length = s // dilation
    half_window = (window // 2) // dilation
    tile = min(tile, length)
    halo_per_tile = tile // half_window
    n_halo = length // half_window
    col_blocks = A_QKV_COLS // WIDTH_A
    view = qa3.reshape(b, length, dilation * A_QKV_COLS)

    def main(sec):
        return pl.BlockSpec((None, tile, WIDTH_A),
                            lambda bi, r, l: (bi, l, r * col_blocks + sec * N_GROUPS_A + group))

    def prev(sec):
        return pl.BlockSpec((None, half_window, WIDTH_A),
                            lambda bi, r, l: (bi, jnp.maximum(l * halo_per_tile - 1, 0),
                                              r * col_blocks + sec * N_GROUPS_A + group))

    def nxt(sec):
        return pl.BlockSpec((None, half_window, WIDTH_A),
                            lambda bi, r, l: (bi, jnp.minimum((l + 1) * halo_per_tile, n_halo - 1),
                                              r * col_blocks + sec * N_GROUPS_A + group))

    out, lse = pl.pallas_call(
        functools.partial(_attn_a_kernel, half_window=half_window, tile=tile, length=length),
        grid=(b, dilation, length // tile),
        in_specs=[main(0), main(1), prev(1), nxt(1), main(2), prev(2), nxt(2)],
        out_specs=[
            pl.BlockSpec((None, tile, WIDTH_A), lambda bi, r, l: (bi, l, r)),
            pl.BlockSpec((None, tile, LANES), lambda bi, r, l: (bi, l, r)),
        ],
        out_shape=[
            jax.ShapeDtypeStruct((b, length, dilation * WIDTH_A), BF16),
            jax.ShapeDtypeStruct((b, length, dilation * LANES), F32),
        ],
        scratch_shapes=[
            pltpu.VMEM((tile + 2 * half_window, WIDTH_A), BF16),
            pltpu.VMEM((tile + 2 * half_window, WIDTH_A), BF16),
        ],
        compiler_params=pltpu.CompilerParams(
            dimension_semantics=("arbitrary", "arbitrary", "arbitrary"),
            vmem_limit_bytes=VMEM_LIMIT),
        name=f"attn_a{group}",
    )(view, view, view, view, view, view, view)
    return out.reshape(b * s, WIDTH_A), lse.reshape(b * s, LANES)


def _attn_b_kernel(sink_ref, q_ref, km_ref, kp_ref, kn_ref, vm_ref, vp_ref, vn_ref, o_ref,
                   kfull, vfull, *, tile, length):
    w = WINDOW_B
    n_keys = Q_BLOCK + 2 * w
    kfull[0:w] = kp_ref[...]
    kfull[w:w + tile] = km_ref[...]
    kfull[w + tile:] = kn_ref[...]
    vfull[0:w] = vp_ref[...]
    vfull[w:w + tile] = vm_ref[...]
    vfull[w + tile:] = vn_ref[...]
    t0 = pl.program_id(1) * tile

    def lane_pair(x):
        swapped = jnp.concatenate([x[:, HEAD_DIM:], x[:, :HEAD_DIM]], axis=1)
        return jnp.concatenate([x, x], axis=1), jnp.concatenate([swapped, swapped], axis=1)

    for i in range(tile // Q_BLOCK):
        r0 = i * Q_BLOCK
        bias = _band_bias(t0 + r0, w, n_keys, length)
        k_even, k_odd = lane_pair(kfull[r0:r0 + n_keys, :])
        v_even, v_odd = lane_pair(vfull[r0:r0 + n_keys, :])
        for c in range(WIDTH_B // SLAB):
            k_ops = [k_even if h % 2 == c else k_odd for h in range(HEADS_PER_SLAB)]
            v_ops = [v_even if h % 2 == c else v_odd for h in range(HEADS_PER_SLAB)]
            sinks = [sink_ref[c * HEADS_PER_SLAB + h] for h in range(HEADS_PER_SLAB)]
            cols = slice(c * SLAB, (c + 1) * SLAB)
            out, _ = _slab_attention(q_ref[r0:r0 + Q_BLOCK, cols], k_ops, v_ops, bias, sinks)
            o_ref[r0:r0 + Q_BLOCK, cols] = out.astype(BF16)


def _attention_b(qb3, sink, tile):
    b, s, _ = qb3.shape
    tile = min(tile, s)
    halo_per_tile = tile // WINDOW_B
    n_halo = s // WINDOW_B
    k_col = WIDTH_B // KV_WIDTH_B
    v_col = k_col + 1

    def main(col):
        return pl.BlockSpec((None, tile, KV_WIDTH_B), lambda bi, l, sk: (bi, l, col))

    def prev(col):
        return pl.BlockSpec((None, WINDOW_B, KV_WIDTH_B),
                            lambda bi, l, sk: (bi, jnp.maximum(l * halo_per_tile - 1, 0), col))

    def nxt(col):
        return pl.BlockSpec((None, WINDOW_B, KV_WIDTH_B),
                            lambda bi, l, sk: (bi, jnp.minimum((l + 1) * halo_per_tile, n_halo - 1), col))

    out = pl.pallas_call(
        functools.partial(_attn_b_kernel, tile=tile, length=s),
        grid_spec=pltpu.PrefetchScalarGridSpec(
            num_scalar_prefetch=1,
            grid=(b, s // tile),
            in_specs=[
                pl.BlockSpec((None, tile, WIDTH_B), lambda bi, l, sk: (bi, l, 0)),
                main(k_col), prev(k_col), nxt(k_col), main(v_col), prev(v_col), nxt(v_col),
            ],
            out_specs=pl.BlockSpec((None, tile, WIDTH_B), lambda bi, l, sk: (bi, l, 0)),
            scratch_shapes=[
                pltpu.VMEM((tile + 2 * WINDOW_B, KV_WIDTH_B), BF16),
                pltpu.VMEM((tile + 2 * WINDOW_B, KV_WIDTH_B), BF16),
            ],
        ),
        out_shape=jax.ShapeDtypeStruct((b, s, WIDTH_B), BF16),
        compiler_params=pltpu.CompilerParams(
            dimension_semantics=("arbitrary", "arbitrary"), vmem_limit_bytes=VMEM_LIMIT),
        name="attn_b",
    )(sink, qb3, qb3, qb3, qb3, qb3, qb3, qb3)
    return out.reshape(b * s, WIDTH_B)


def _sigmoid(x):
    return 1.0 / (1.0 + jnp.exp(-x))


def _merge_kernel(x_ref, p_ref, o0_ref, o1_ref, o2_ref, l0_ref, l1_ref, l2_ref, bo_ref, g_ref,
                  wa_ref, wb_ref, wo_ref, wg_ref, wp_ref, np_ref, ex_ref, y_ref):
    l0 = l0_ref[...]
    l1 = l1_ref[...]
    l2 = l2_ref[...]
    mx = jnp.maximum(jnp.maximum(l0, l1), l2)
    e0 = jnp.exp(l0 - mx)
    e1 = jnp.exp(l1 - mx)
    e2 = jnp.exp(l2 - mx)
    inv = 1.0 / (e0 + e1 + e2)
    ex = ex_ref[...]

    def spread(wt):
        return jnp.dot(wt.astype(BF16), ex, preferred_element_type=F32)

    a_out = (spread(e0 * inv) * o0_ref[...].astype(F32)
             + spread(e1 * inv) * o1_ref[...].astype(F32)
             + spread(e2 * inv) * o2_ref[...].astype(F32))
    a_gate = g_ref[:, 0:512].astype(F32)
    b_gate = g_ref[:, 512:1024].astype(F32)
    a_in = (a_out * (a_gate * _sigmoid(a_gate))).astype(BF16)
    b_in = (bo_ref[...].astype(F32) * (b_gate * _sigmoid(b_gate))).astype(BF16)
    a_y = jnp.dot(a_in, wa_ref[...], preferred_element_type=F32)
    b_y = jnp.dot(b_in, wb_ref[...], preferred_element_type=F32)
    merged = (_sigmoid(g_ref[:, 1024:2048].astype(F32)) * a_y
              + _sigmoid(g_ref[:, 2048:3072].astype(F32)) * b_y)
    x1 = x_ref[...] + jnp.dot(merged.astype(BF16), wo_ref[...], preferred_element_type=F32)
    ms = jnp.mean(x1 * x1, axis=-1, keepdims=True)
    xn = (x1 * lax.rsqrt(ms + EPS) * np_ref[...]).astype(BF16)
    gate = _sigmoid(jnp.dot(xn, wg_ref[...], preferred_element_type=F32))
    ple = jnp.dot(p_ref[...].astype(BF16), wp_ref[...], preferred_element_type=F32)
    y_ref[...] = x1 + gate * ple


def _merge(x2, p2, o0, o1, o2, l0, l1, l2, bo, g, wa, wb, wo, wg, wp, npl, ex, tile):
    n = x2.shape[0]
    row = lambda i: (i, 0)
    const = lambda i: (0, 0)

    def rows(width):
        return pl.BlockSpec((tile, width), row)

    def whole(r, c):
        return pl.BlockSpec((r, c), const, pipeline_mode=pl.Buffered(1))

    return pl.pallas_call(
        _merge_kernel,
        grid=(n // tile,),
        in_specs=[
            rows(D_MODEL), rows(PLE_DIM), rows(WIDTH_A), rows(WIDTH_A), rows(WIDTH_A),
            rows(LANES), rows(LANES), rows(LANES), rows(WIDTH_B), rows(3 * D_MODEL),
            whole(WIDTH_A, D_MODEL), whole(WIDTH_B, D_MODEL), whole(D_MODEL, D_MODEL),
            whole(D_MODEL, D_MODEL), whole(PLE_DIM, D_MODEL), whole(1, D_MODEL),
            whole(LANES, WIDTH_A),
        ],
        out_specs=rows(D_MODEL),
        out_shape=jax.ShapeDtypeStruct((n, D_MODEL), F32),
        compiler_params=pltpu.CompilerParams(
            dimension_semantics=("arbitrary",), vmem_limit_bytes=VMEM_LIMIT),
        name="merge_out",
    )(x2, p2, o0, o1, o2, l0, l1, l2, bo, g, wa, wb, wo, wg, wp, npl, ex)


def _rope_tables(s):
    inv = ROPE_THETA ** (-jnp.arange(0, ROT_DIM, 2, dtype=F32) / ROT_DIM)
    ang = jnp.arange(s, dtype=F32)[:, None] * inv[None, :]
    ang = jnp.concatenate([ang, ang], axis=-1)
    cos = jnp.cos(ang)
    sin = jnp.sin(ang)
    half = ROT_DIM // 2
    pad = HEAD_DIM - ROT_DIM
    c_head = jnp.concatenate([cos, jnp.ones((s, pad), F32)], axis=1)
    s1_head = jnp.concatenate([-sin[:, :half], jnp.zeros((s, HEAD_DIM - half), F32)], axis=1)
    s2_head = jnp.concatenate([jnp.zeros((s, half), F32), sin[:, half:], jnp.zeros((s, pad), F32)], axis=1)
    reps = LANES // HEAD_DIM
    return jnp.tile(c_head, (1, reps)), jnp.tile(s1_head, (1, reps)), jnp.tile(s2_head, (1, reps))


def _layer_constants():
    lane = jnp.arange(SLAB)
    ones = (lane[:, None] // HEAD_DIM == lane[None, :] // HEAD_DIM).astype(BF16)
    spread = (jnp.arange(LANES)[:, None] == jnp.arange(WIDTH_A)[None, :] // HEAD_DIM).astype(BF16)
    return ones, spread


def _trunk(x, p, norm_mix, w_in, a_q_norm, a_k_norm, b_q_norm, b_k_norm, b_sink, w_branch_a,
           w_branch_b, w_out, norm_ple, w_ple, w_ple_gate, *, row_tile=256, attn_tile=512):
    b, s, _ = x.shape
    depth = w_in.shape[0]
    x2 = x.reshape(b * s, D_MODEL)
    c_t, s1_t, s2_t = _rope_tables(s)
    ones, spread = _layer_constants()
    scale = HEAD_DIM ** -0.5
    for i in range(depth):
        heads = WIDTH_A // HEAD_DIM
        ga = jnp.concatenate([jnp.tile(a_q_norm[i] * scale, (1, heads)).reshape(1, -1),
                              jnp.tile(a_k_norm[i], (1, heads)).reshape(1, -1)], axis=1)
        gb = jnp.concatenate([jnp.tile(b_q_norm[i] * scale, N_HEADS_B),
                              jnp.tile(b_k_norm[i], KV_WIDTH_B // HEAD_DIM)])[None, :]
        qa, qb, g = _project(x2, s, norm_mix[i][None, :], w_in[i].astype(BF16), ga, gb,
                             c_t, s1_t, s2_t, ones, row_tile)
        qa3 = qa.reshape(b, s, A_QKV_COLS)
        outs = [_attention_a(qa3, gi, dil, window, attn_tile)
                for gi, (window, dil) in enumerate(DIL_PAIRS)]
        bo = _attention_b(qb.reshape(b, s, WIDTH_B + 2 * KV_WIDTH_B), b_sink[i], attn_tile)
        x2 = _merge(x2, p[i].reshape(b * s, PLE_DIM), outs[0][0], outs[1][0], outs[2][0],
                    outs[0][1], outs[1][1], outs[2][1], bo, g,
                    w_branch_a[i].astype(BF16), w_branch_b[i].astype(BF16), w_out[i].astype(BF16),
                    w_ple_gate[i].astype(BF16), w_ple[i].astype(BF16), norm_ple[i][None, :],
                    spread, row_tile)
    return x2.reshape(b, s, D_MODEL)


def kernel(x_prompt, x_sample, p_prompt, p_sample, norm_mix, w_in, a_q_norm, a_k_norm, b_q_norm,
           b_k_norm, b_sink, w_branch_a, w_branch_b, w_out, norm_ple, w_ple, w_ple_gate):
    weights = (norm_mix, w_in, a_q_norm, a_k_norm, b_q_norm, b_k_norm, b_sink, w_branch_a,
               w_branch_b, w_out, norm_ple, w_ple, w_ple_gate)
    return (_trunk(x_prompt, p_prompt, *weights), _trunk(x_sample, p_sample, *weights))
```

```python
import functools

import jax
import jax.numpy as jnp
from jax import lax
from jax.experimental import pallas as pl
from jax.experimental.pallas import tpu as pltpu

D_MODEL = 1024
HEAD_DIM = 64
ROT_DIM = HEAD_DIM // 4
ROPE_THETA = 500000.0
DIL_PAIRS = ((128, 1), (512, 4), (2048, 16))
N_GROUPS_A = 3
WIDTH_A = 512
WIDTH_B = 512
KV_WIDTH_B = 128
N_HEADS_B = 8
WINDOW_B = 128
PLE_DIM = 256
A_QKV_COLS = 3 * N_GROUPS_A * WIDTH_A
IN_COLS = A_QKV_COLS + WIDTH_A + WIDTH_B + 2 * KV_WIDTH_B + WIDTH_B + 2 * D_MODEL
EPS = 1e-6
NEG = -1e30

OFF_A_GATE = A_QKV_COLS
OFF_BQ = OFF_A_GATE + WIDTH_A
OFF_BK = OFF_BQ + WIDTH_B
OFF_BV = OFF_BK + KV_WIDTH_B
OFF_B_GATE = OFF_BV + KV_WIDTH_B
OFF_MG_A = OFF_B_GATE + WIDTH_B
OFF_MG_B = OFF_MG_A + D_MODEL

LANES = 128
SLAB = 256
HEADS_PER_SLAB = SLAB // HEAD_DIM
Q_BLOCK = 128
ROW_TILE = 512
ATTN_TILE = 512
VMEM_LIMIT = 56 * 1024 * 1024

BF16 = jnp.bfloat16
F32 = jnp.float32


def _residue_major(ref, dilation):
    rows = ref.shape[0] // dilation
    return jnp.concatenate([ref[pl.ds(r, rows, stride=dilation), :] for r in range(dilation)], axis=0)


def _proj_kernel(x_ref, nm_ref, w_ref, ga_ref, gb_ref, c_ref, s1_ref, s2_ref, ones_ref,
                 qa0_ref, qa1_ref, qa2_ref, qb_ref, g_ref, hslab):
    xf = x_ref[...]
    ms = jnp.mean(xf * xf, axis=-1, keepdims=True)
    hn32 = xf * lax.rsqrt(ms + EPS) * nm_ref[...]
    n_slabs = D_MODEL // LANES
    for j in range(n_slabs):
        hslab[j] = hn32[:, j * LANES:(j + 1) * LANES]
    ones = ones_ref[...]

    def hn_for(dilation):
        if dilation == 1:
            return hn32.astype(BF16)
        return jnp.concatenate([_residue_major(hslab.at[j], dilation) for j in range(n_slabs)],
                               axis=1).astype(BF16)

    def tables_for(dilation):
        if dilation == 1:
            return c_ref[...], s1_ref[...], s2_ref[...]
        return tuple(_residue_major(t, dilation) for t in (c_ref, s1_ref, s2_ref))

    def wide(t):
        return jnp.concatenate([t, t], axis=1)

    def norm_rope(t, gain, cc, sa, sb, ones_m):
        width = t.shape[1]
        ss = jnp.dot((t * t).astype(BF16), ones_m, preferred_element_type=F32)
        n = t * lax.rsqrt(ss * (1.0 / HEAD_DIM) + EPS) * gain
        up = pltpu.roll(n, width - ROT_DIM // 2, 1)
        dn = pltpu.roll(n, ROT_DIM // 2, 1)
        return n * cc + up * sa + dn * sb

    def proj(hn, lo, width):
        return jnp.dot(hn, w_ref[:, lo:lo + width], preferred_element_type=F32)

    def normed(hn, lo, gains, glo, tabs):
        t = proj(hn, lo, WIDTH_A)
        cc, sa, sb = (wide(t_) for t_ in tabs)
        halves = [norm_rope(t[:, h * SLAB:(h + 1) * SLAB], gains[:, glo + h * SLAB:glo + (h + 1) * SLAB],
                            cc, sa, sb, ones) for h in range(WIDTH_A // SLAB)]
        return jnp.concatenate(halves, axis=1).astype(BF16)

    for gi, (out_ref, (_, dilation)) in enumerate(zip((qa0_ref, qa1_ref, qa2_ref), DIL_PAIRS)):
        hn = hn_for(dilation)
        tabs = tables_for(dilation)
        q = normed(hn, gi * WIDTH_A, ga_ref, gi * WIDTH_A, tabs)
        k = normed(hn, (N_GROUPS_A + gi) * WIDTH_A, ga_ref, (N_GROUPS_A + gi) * WIDTH_A, tabs)
        v = proj(hn, (2 * N_GROUPS_A + gi) * WIDTH_A, WIDTH_A).astype(BF16)
        rows = out_ref.shape[1]
        for r in range(dilation):
            rs = slice(r * rows, (r + 1) * rows)
            out_ref[r, :, 0:WIDTH_A] = q[rs]
            out_ref[r, :, WIDTH_A:2 * WIDTH_A] = k[rs]
            out_ref[r, :, 2 * WIDTH_A:] = v[rs]

    hn = hn_for(1)
    tabs = tables_for(1)
    qb_ref[:, 0:WIDTH_B] = normed(hn, OFF_BQ, gb_ref, 0, tabs)
    t = proj(hn, OFF_BK, 2 * KV_WIDTH_B)
    kb = norm_rope(t[:, :KV_WIDTH_B], gb_ref[:, WIDTH_B:WIDTH_B + KV_WIDTH_B], tabs[0], tabs[1], tabs[2],
                   ones[:KV_WIDTH_B, :KV_WIDTH_B])
    qb_ref[:, WIDTH_B:WIDTH_B + KV_WIDTH_B] = kb.astype(BF16)
    qb_ref[:, WIDTH_B + KV_WIDTH_B:] = t[:, KV_WIDTH_B:].astype(BF16)
    g_ref[:, 0:512] = proj(hn, OFF_A_GATE, 512).astype(BF16)
    g_ref[:, 512:1024] = proj(hn, OFF_B_GATE, 512).astype(BF16)
    for j in range(2 * D_MODEL // 512):
        g_ref[:, 1024 + j * 512:1536 + j * 512] = proj(hn, OFF_MG_A + j * 512, 512).astype(BF16)


def _project(x3, nm, w, ga, gb, c_t, s1_t, s2_t, ones, tile):
    b, s, _ = x3.shape
    const = lambda bi, i: (0, 0)
    tab = lambda bi, i: (i, 0)
    row = lambda bi, i: (bi, i, 0)
    qa_specs, qa_shapes = [], []
    for _, d in DIL_PAIRS:
        qa_specs.append(pl.BlockSpec((None, d, tile // d, 3 * WIDTH_A), lambda bi, i: (bi, 0, i, 0)))
        qa_shapes.append(jax.ShapeDtypeStruct((b, d, s // d, 3 * WIDTH_A), BF16))
    return pl.pallas_call(
        _proj_kernel,
        grid=(b, s // tile),
        in_specs=[
            pl.BlockSpec((None, tile, D_MODEL), row),
            pl.BlockSpec((1, D_MODEL), const),
            pl.BlockSpec((D_MODEL, IN_COLS), const, pipeline_mode=pl.Buffered(1)),
            pl.BlockSpec((1, 2 * N_GROUPS_A * WIDTH_A), const),
            pl.BlockSpec((1, WIDTH_B + KV_WIDTH_B), const),
            pl.BlockSpec((tile, LANES), tab),
            pl.BlockSpec((tile, LANES), tab),
            pl.BlockSpec((tile, LANES), tab),
            pl.BlockSpec((SLAB, SLAB), const),
        ],
        out_specs=qa_specs + [
            pl.BlockSpec((None, tile, WIDTH_B + 2 * KV_WIDTH_B), row),
            pl.BlockSpec((None, tile, 3 * D_MODEL), row),
        ],
        out_shape=qa_shapes + [
            jax.ShapeDtypeStruct((b, s, WIDTH_B + 2 * KV_WIDTH_B), BF16),
            jax.ShapeDtypeStruct((b, s, 3 * D_MODEL), BF16),
        ],
        scratch_shapes=[pltpu.VMEM((D_MODEL // LANES, tile, LANES), F32)],
        compiler_params=pltpu.CompilerParams(
            dimension_semantics=("arbitrary", "arbitrary"), vmem_limit_bytes=VMEM_LIMIT),
        name="proj_in",
    )(x3, nm, w, ga, gb, c_t, s1_t, s2_t, ones)


def _band_bias(base, half_window, n_keys, length):
    qi = lax.broadcasted_iota(jnp.int32, (Q_BLOCK, 1), 0)
    jj = lax.broadcasted_iota(jnp.int32, (Q_BLOCK, n_keys), 1)
    lo = jnp.maximum(qi, half_window - base)
    hi = jnp.minimum(qi + 2 * half_window, length - 1 - base + half_window)
    return jnp.where(jj >= lo, jnp.where(jj <= hi, 0.0, NEG), NEG).astype(F32)


def _slab_attention(q_slab, kw, vw, bias, sinks):
    head_of_lane = lax.broadcasted_iota(jnp.int32, (1, SLAB), 1) // HEAD_DIM
    zero = jnp.zeros((), BF16)
    heads = range(HEADS_PER_SLAB)
    q4 = jnp.concatenate([jnp.where(head_of_lane == h, q_slab, zero) for h in heads], axis=0)
    s4 = lax.dot_general(q4, kw, (((1,), (1,)), ((), ())), preferred_element_type=F32)
    probs, stats = [], []
    for h in heads:
        s = s4[h * Q_BLOCK:(h + 1) * Q_BLOCK] + bias
        m = jnp.max(s, axis=-1, keepdims=True)
        if sinks is not None:
            m = jnp.maximum(m, sinks[h])
        e = jnp.exp(s - m)
        l = jnp.sum(e, axis=-1, keepdims=True)
        if sinks is not None:
            l = l + jnp.exp(sinks[h] - m)
        probs.append(e.astype(BF16))
        stats.append((m, l))
    p4 = jnp.concatenate(probs, axis=1)
    v4 = jnp.concatenate([jnp.where(head_of_lane == h, vw, zero) for h in heads], axis=0)
    acc = jnp.dot(p4, v4, preferred_element_type=F32)
    inv = 1.0 / stats[HEADS_PER_SLAB - 1][1]
    for h in range(HEADS_PER_SLAB - 2, -1, -1):
        inv = jnp.where(head_of_lane == h, 1.0 / stats[h][1], inv)
    return acc * inv, stats


def _fill_window(full, prev_ref, main_ref, next_ref, w, tile):
    full[0:w] = prev_ref[...]
    full[w:w + tile] = main_ref[...]
    full[w + tile:] = next_ref[...]


def _attn_a_kernel(q_ref, km_ref, kp_ref, kn_ref, vm_ref, vp_ref, vn_ref, o_ref, lse_ref,
                   kfull, vfull, *, half_window, tile, length):
    w = half_window
    n_keys = Q_BLOCK + 2 * w
    _fill_window(kfull, kp_ref, km_ref, kn_ref, w, tile)
    _fill_window(vfull, vp_ref, vm_ref, vn_ref, w, tile)
    t0 = pl.program_id(2) * tile
    lane = lax.broadcasted_iota(jnp.int32, (1, LANES), 1)
    for i in range(tile // Q_BLOCK):
        r0 = i * Q_BLOCK
        bias = _band_bias(t0 + r0, w, n_keys, length)
        lse_rows = jnp.zeros((Q_BLOCK, LANES), F32)
        for c in range(WIDTH_A // SLAB):
            cols = slice(c * SLAB, (c + 1) * SLAB)
            out, stats = _slab_attention(q_ref[r0:r0 + Q_BLOCK, cols], kfull[r0:r0 + n_keys, cols],
                                         vfull[r0:r0 + n_keys, cols], bias, None)
            o_ref[r0:r0 + Q_BLOCK, cols] = out.astype(BF16)
            for h in range(HEADS_PER_SLAB):
                m, l = stats[h]
                lse_rows = jnp.where(lane == c * HEADS_PER_SLAB + h, m + jnp.log(l), lse_rows)
        lse_ref[r0:r0 + Q_BLOCK, :] = lse_rows


def _attention_a(qkv, window, tile):
    b, dilation, length, _ = qkv.shape
    half_window = (window // 2) // dilation
    tile = min(tile, length)
    halo_per_tile = tile // half_window
    n_halo = length // half_window

    def main(sec):
        return pl.BlockSpec((None, None, tile, WIDTH_A), lambda bi, r, l: (bi, r, l, sec))

    def prev(sec):
        return pl.BlockSpec((None, None, half_window, WIDTH_A),
                            lambda bi, r, l: (bi, r, jnp.maximum(l * halo_per_tile - 1, 0), sec))

    def nxt(sec):
        return pl.BlockSpec((None, None, half_window, WIDTH_A),
                            lambda bi, r, l: (bi, r, jnp.minimum((l + 1) * halo_per_tile, n_halo - 1), sec))

    return pl.pallas_call(
        functools.partial(_attn_a_kernel, half_window=half_window, tile=tile, length=length),
        grid=(b, dilation, length // tile),
        in_specs=[main(0), main(1), prev(1), nxt(1), main(2), prev(2), nxt(2)],
        out_specs=[
            pl.BlockSpec((None, None, tile, WIDTH_A), lambda bi, r, l: (bi, r, l, 0)),
            pl.BlockSpec((None, None, tile, LANES), lambda bi, r, l: (bi, r, l, 0)),
        ],
        out_shape=[
            jax.ShapeDtypeStruct((b, dilation, length, WIDTH_A), BF16),
            jax.ShapeDtypeStruct((b, dilation, length, LANES), F32),
        ],
        scratch_shapes=[
            pltpu.VMEM((tile + 2 * half_window, WIDTH_A), BF16),
            pltpu.VMEM((tile + 2 * half_window, WIDTH_A), BF16),
        ],
        compiler_params=pltpu.CompilerParams(
            dimension_semantics=("arbitrary", "arbitrary", "arbitrary"),
            vmem_limit_bytes=VMEM_LIMIT),
        name=f"attn_a_d{dilation}",
    )(qkv, qkv, qkv, qkv, qkv, qkv, qkv)


def _attn_b_kernel(sink_ref, q_ref, km_ref, kp_ref, kn_ref, vm_ref, vp_ref, vn_ref, o_ref,
                   kfull, vfull, *, tile, length):
    w = WINDOW_B
    n_keys = Q_BLOCK + 2 * w
    _fill_window(kfull, kp_ref, km_ref, kn_ref, w, tile)
    _fill_window(vfull, vp_ref, vm_ref, vn_ref, w, tile)
    t0 = pl.program_id(1) * tile
    even_block = (lax.broadcasted_iota(jnp.int32, (1, SLAB), 1) // HEAD_DIM) % 2 == 0

    def per_kv_head(x):
        swapped = jnp.concatenate([x[:, HEAD_DIM:], x[:, :HEAD_DIM]], axis=1)
        pair = jnp.concatenate([x, x], axis=1)
        riap = jnp.concatenate([swapped, swapped], axis=1)
        return jnp.where(even_block, pair, riap), jnp.where(even_block, riap, pair)

    for i in range(tile // Q_BLOCK):
        r0 = i * Q_BLOCK
        bias = _band_bias(t0 + r0, w, n_keys, length)
        k_heads = per_kv_head(kfull[r0:r0 + n_keys, :])
        v_heads = per_kv_head(vfull[r0:r0 + n_keys, :])
        for c in range(WIDTH_B // SLAB):
            sinks = [sink_ref[c * HEADS_PER_SLAB + h] for h in range(HEADS_PER_SLAB)]
            cols = slice(c * SLAB, (c + 1) * SLAB)
            out, _ = _slab_attention(q_ref[r0:r0 + Q_BLOCK, cols], k_heads[c], v_heads[c], bias, sinks)
            o_ref[r0:r0 + Q_BLOCK, cols] = out.astype(BF16)


def _attention_b(qb3, sink, tile):
    b, s, _ = qb3.shape
    tile = min(tile, s)
    halo_per_tile = tile // WINDOW_B
    n_halo = s // WINDOW_B
    k_col = WIDTH_B // KV_WIDTH_B
    v_col = k_col + 1

    def main(col):
        return pl.BlockSpec((None, tile, KV_WIDTH_B), lambda bi, l, sk: (bi, l, col))

    def prev(col):
        return pl.BlockSpec((None, WINDOW_B, KV_WIDTH_B),
                            lambda bi, l, sk: (bi, jnp.maximum(l * halo_per_tile - 1, 0), col))

    def nxt(col):
        return pl.BlockSpec((None, WINDOW_B, KV_WIDTH_B),
                            lambda bi, l, sk: (bi, jnp.minimum((l + 1) * halo_per_tile, n_halo - 1), col))

    return pl.pallas_call(
        functools.partial(_attn_b_kernel, tile=tile, length=s),
        grid_spec=pltpu.PrefetchScalarGridSpec(
            num_scalar_prefetch=1,
            grid=(b, s // tile),
            in_specs=[
                pl.BlockSpec((None, tile, WIDTH_B), lambda bi, l, sk: (bi, l, 0)),
                main(k_col), prev(k_col), nxt(k_col), main(v_col), prev(v_col), nxt(v_col),
            ],
            out_specs=pl.BlockSpec((None, tile, WIDTH_B), lambda bi, l, sk: (bi, l, 0)),
            scratch_shapes=[
                pltpu.VMEM((tile + 2 * WINDOW_B, KV_WIDTH_B), BF16),
                pltpu.VMEM((tile + 2 * WINDOW_B, KV_WIDTH_B), BF16),
            ],
        ),
        out_shape=jax.ShapeDtypeStruct((b, s, WIDTH_B), BF16),
        compiler_params=pltpu.CompilerParams(
            dimension_semantics=("arbitrary", "arbitrary"), vmem_limit_bytes=VMEM_LIMIT),
        name="attn_b",
    )(sink, qb3, qb3, qb3, qb3, qb3, qb3, qb3)


def _sigmoid(x):
    return 1.0 / (1.0 + jnp.exp(-x))


def _merge_kernel(x_ref, p_ref, o0_ref, o1_ref, o2_ref, l0_ref, l1_ref, l2_ref, bo_ref, g_ref,
                  wa_ref, wb_ref, wo_ref, wg_ref, wp_ref, np_ref, ex_ref, y_ref, oslab, lslab):
    n_slabs = WIDTH_A // LANES

    def token_major(o_ref, l_ref, dilation, slot):
        if dilation == 1:
            return o_ref[0].astype(F32), l_ref[0]
        rows = o_ref.shape[1]
        for r in range(dilation):
            o_r = o_ref[r].astype(F32)
            for j in range(n_slabs):
                oslab[slot, j, pl.ds(r, rows, stride=dilation), :] = o_r[:, j * LANES:(j + 1) * LANES]
            lslab[slot, pl.ds(r, rows, stride=dilation), :] = l_ref[r]
        return jnp.concatenate([oslab[slot, j] for j in range(n_slabs)], axis=1), lslab[slot]

    groups = [token_major(o_ref, l_ref, d, slot) for slot, (o_ref, l_ref, (_, d)) in
              enumerate(zip((o0_ref, o1_ref, o2_ref), (l0_ref, l1_ref, l2_ref), DIL_PAIRS))]
    l0, l1, l2 = (g[1] for g in groups)
    mx = jnp.maximum(jnp.maximum(l0, l1), l2)
    e0 = jnp.exp(l0 - mx)
    e1 = jnp.exp(l1 - mx)
    e2 = jnp.exp(l2 - mx)
    inv = 1.0 / (e0 + e1 + e2)
    ex = ex_ref[...]

    def spread(wt):
        return jnp.dot(wt.astype(BF16), ex, preferred_element_type=F32)

    a_out = (spread(e0 * inv) * groups[0][0] + spread(e1 * inv) * groups[1][0]
             + spread(e2 * inv) * groups[2][0])
    a_gate = g_ref[:, 0:512].astype(F32)
    b_gate = g_ref[:, 512:1024].astype(F32)
    a_in = (a_out * (a_gate * _sigmoid(a_gate))).astype(BF16)
    b_in = (bo_ref[...].astype(F32) * (b_gate * _sigmoid(b_gate))).astype(BF16)
    a_y = jnp.dot(a_in, wa_ref[...], preferred_element_type=F32)
    b_y = jnp.dot(b_in, wb_ref[...], preferred_element_type=F32)
    merged = (_sigmoid(g_ref[:, 1024:2048].astype(F32)) * a_y
              + _sigmoid(g_ref[:, 2048:3072].astype(F32)) * b_y)
    x1 = x_ref[...] + jnp.dot(merged.astype(BF16), wo_ref[...], preferred_element_type=F32)
    ms = jnp.mean(x1 * x1, axis=-1, keepdims=True)
    xn = (x1 * lax.rsqrt(ms + EPS) * np_ref[...]).astype(BF16)
    gate = _sigmoid(jnp.dot(xn, wg_ref[...], preferred_element_type=F32))
    ple = jnp.dot(p_ref[...].astype(BF16), wp_ref[...], preferred_element_type=F32)
    y_ref[...] = x1 + gate * ple


def _merge(x3, p3, outs, bo, g, wa, wb, wo, wg, wp, npl, ex, tile):
    b, s, _ = x3.shape
    row = lambda bi, i: (bi, i, 0)
    const = lambda bi, i: (0, 0)

    def rows(width):
        return pl.BlockSpec((None, tile, width), row)

    def grouped(width, d):
        return pl.BlockSpec((None, d, tile // d, width), lambda bi, i: (bi, 0, i, 0))

    def whole(r, c):
        return pl.BlockSpec((r, c), const, pipeline_mode=pl.Buffered(1))

    dils = [d for _, d in DIL_PAIRS]
    return pl.pallas_call(
        _merge_kernel,
        grid=(b, s // tile),
        in_specs=[rows(D_MODEL), rows(PLE_DIM)]
        + [grouped(WIDTH_A, d) for d in dils] + [grouped(LANES, d) for d in dils]
        + [rows(WIDTH_B), rows(3 * D_MODEL),
           whole(WIDTH_A, D_MODEL), whole(WIDTH_B, D_MODEL), whole(D_MODEL, D_MODEL),
           whole(D_MODEL, D_MODEL), whole(PLE_DIM, D_MODEL), whole(1, D_MODEL),
           whole(LANES, WIDTH_A)],
        out_specs=rows(D_MODEL),
        out_shape=jax.ShapeDtypeStruct((b, s, D_MODEL), F32),
        scratch_shapes=[
            pltpu.VMEM((N_GROUPS_A, WIDTH_A // LANES, tile, LANES), F32),
            pltpu.VMEM((N_GROUPS_A, tile, LANES), F32),
        ],
        compiler_params=pltpu.CompilerParams(
            dimension_semantics=("arbitrary", "arbitrary"), vmem_limit_bytes=VMEM_LIMIT),
        name="merge_out",
    )(x3, p3, *[o for o, _ in outs], *[l for _, l in outs], bo, g, wa, wb, wo, wg, wp, npl, ex)


def _rope_tables(s):
    inv = ROPE_THETA ** (-jnp.arange(0, ROT_DIM, 2, dtype=F32) / ROT_DIM)
    ang = jnp.arange(s, dtype=F32)[:, None] * inv[None, :]
    ang = jnp.concatenate([ang, ang], axis=-1)
    cos = jnp.cos(ang)
    sin = jnp.sin(ang)
    half = ROT_DIM // 2
    pad = HEAD_DIM - ROT_DIM
    c_head = jnp.concatenate([cos, jnp.ones((s, pad), F32)], axis=1)
    s1_head = jnp.concatenate([-sin[:, :half], jnp.zeros((s, HEAD_DIM - half), F32)], axis=1)
    s2_head = jnp.concatenate([jnp.zeros((s, half), F32), sin[:, half:], jnp.zeros((s, pad), F32)], axis=1)
    reps = LANES // HEAD_DIM
    return jnp.tile(c_head, (1, reps)), jnp.tile(s1_head, (1, reps)), jnp.tile(s2_head, (1, reps))


def _layer_constants():
    lane = jnp.arange(SLAB)
    ones = (lane[:, None] // HEAD_DIM == lane[None, :] // HEAD_DIM).astype(BF16)
    spread = (jnp.arange(LANES)[:, None] == jnp.arange(WIDTH_A)[None, :] // HEAD_DIM).astype(BF16)
    return ones, spread


def _trunk(x, p, norm_mix, w_in, a_q_norm, a_k_norm, b_q_norm, b_k_norm, b_sink, w_branch_a,
           w_branch_b, w_out, norm_ple, w_ple, w_ple_gate):
    b, s, _ = x.shape
    depth = w_in.shape[0]
    row_tile = min(ROW_TILE, s)
    c_t, s1_t, s2_t = _rope_tables(s)
    ones, spread = _layer_constants()
    scale = HEAD_DIM ** -0.5
    heads = WIDTH_A // HEAD_DIM
    for i in range(depth):
        ga = jnp.concatenate([jnp.tile(a_q_norm[i] * scale, (1, heads)).reshape(1, -1),
                              jnp.tile(a_k_norm[i], (1, heads)).reshape(1, -1)], axis=1)
        gb = jnp.concatenate([jnp.tile(b_q_norm[i] * scale, N_HEADS_B),
                              jnp.tile(b_k_norm[i], KV_WIDTH_B // HEAD_DIM)])[None, :]
        qa0, qa1, qa2, qb, g = _project(x, norm_mix[i][None, :], w_in[i].astype(BF16), ga, gb,
                                        c_t, s1_t, s2_t, ones, row_tile)
        outs = [_attention_a(qkv, window, ATTN_TILE)
                for qkv, (window, _) in zip((qa0, qa1, qa2), DIL_PAIRS)]
        bo = _attention_b(qb, b_sink[i], ATTN_TILE)
        x = _merge(x, p[i], outs, bo, g,
                   w_branch_a[i].astype(BF16), w_branch_b[i].astype(BF16), w_out[i].astype(BF16),
                   w_ple_gate[i].astype(BF16), w_ple[i].astype(BF16), norm_ple[i][None, :],
                   spread, row_tile)
    return x


def kernel(x_prompt, x_sample, p_prompt, p_sample, norm_mix, w_in, a_q_norm, a_k_norm, b_q_norm,
           b_k_norm, b_sink, w_branch_a, w_branch_b, w_out, norm_ple, w_ple, w_ple_gate):
    weights = (norm_mix, w_in, a_q_norm, a_k_norm, b_q_norm, b_k_norm, b_sink, w_branch_a,
               w_branch_b, w_out, norm_ple, w_ple, w_ple_gate)
    return (_trunk(x_prompt, p_prompt, *weights), _trunk(x_sample, p_sample, *weights))
```

```python
import functools

import jax
import jax.numpy as jnp
from jax import lax
from jax.experimental import pallas as pl
from jax.experimental.pallas import tpu as pltpu

D_MODEL = 1024
HEAD_DIM = 64
ROT_DIM = HEAD_DIM // 4
ROPE_THETA = 500000.0
DIL_PAIRS = ((128, 1), (512, 4), (2048, 16))
N_GROUPS_A = 3
WIDTH_A = 512
WIDTH_B = 512
KV_WIDTH_B = 128
N_HEADS_B = 8
WINDOW_B = 128
PLE_DIM = 256
A_QKV_COLS = 3 * N_GROUPS_A * WIDTH_A
IN_COLS = A_QKV_COLS + WIDTH_A + WIDTH_B + 2 * KV_WIDTH_B + WIDTH_B + 2 * D_MODEL
EPS = 1e-6
NEG = -1e30
LOG2E = 1.4426950408889634

OFF_A_GATE = A_QKV_COLS
OFF_BQ = OFF_A_GATE + WIDTH_A
OFF_BK = OFF_BQ + WIDTH_B
OFF_BV = OFF_BK + KV_WIDTH_B
OFF_B_GATE = OFF_BV + KV_WIDTH_B
OFF_MG_A = OFF_B_GATE + WIDTH_B
OFF_MG_B = OFF_MG_A + D_MODEL

LANES = 128
SLAB = 256
HEADS_PER_SLAB = SLAB // HEAD_DIM
Q_BLOCK = 128
ROW_TILE = 512
ATTN_TILE = 1024
VMEM_LIMIT = 56 * 1024 * 1024

BF16 = jnp.bfloat16
F32 = jnp.float32


def _residue_major(ref, dilation):
    rows = ref.shape[0] // dilation
    return jnp.concatenate([ref[pl.ds(r, rows, stride=dilation), :] for r in range(dilation)], axis=0)


def _proj_kernel(x_ref, nm_ref, w_ref, ga_ref, gb_ref, c_ref, s1_ref, s2_ref, ones_ref,
                 qa0_ref, qa1_ref, qa2_ref, qb_ref, g_ref, hslab):
    xf = x_ref[...]
    ms = jnp.mean(xf * xf, axis=-1, keepdims=True)
    hn32 = xf * lax.rsqrt(ms + EPS) * nm_ref[...]
    n_slabs = D_MODEL // LANES
    for j in range(n_slabs):
        hslab[j] = hn32[:, j * LANES:(j + 1) * LANES]
    ones = ones_ref[...]

    def hn_for(dilation):
        if dilation == 1:
            return hn32.astype(BF16)
        return jnp.concatenate([_residue_major(hslab.at[j], dilation) for j in range(n_slabs)],
                               axis=1).astype(BF16)

    def tables_for(dilation):
        if dilation == 1:
            return c_ref[...], s1_ref[...], s2_ref[...]
        return tuple(_residue_major(t, dilation) for t in (c_ref, s1_ref, s2_ref))

    def wide(t):
        return jnp.concatenate([t, t], axis=1)

    def norm_rope(t, gain, cc, sa, sb, ones_m):
        width = t.shape[1]
        ms = jnp.dot((t * t).astype(BF16), ones_m, preferred_element_type=F32)
        n = t * lax.rsqrt(ms + EPS) * gain
        up = pltpu.roll(n, width - ROT_DIM // 2, 1)
        dn = pltpu.roll(n, ROT_DIM // 2, 1)
        return n * cc + up * sa + dn * sb

    def proj(hn, lo, width):
        return jnp.dot(hn, w_ref[:, lo:lo + width], preferred_element_type=F32)

    def normed(hn, lo, gains, glo, tabs):
        t = proj(hn, lo, WIDTH_A)
        cc, sa, sb = (wide(t_) for t_ in tabs)
        halves = [norm_rope(t[:, h * SLAB:(h + 1) * SLAB], gains[:, glo + h * SLAB:glo + (h + 1) * SLAB],
                            cc, sa, sb, ones) for h in range(WIDTH_A // SLAB)]
        return jnp.concatenate(halves, axis=1).astype(BF16)

    for gi, (out_ref, (_, dilation)) in enumerate(zip((qa0_ref, qa1_ref, qa2_ref), DIL_PAIRS)):
        hn = hn_for(dilation)
        tabs = tables_for(dilation)
        q = normed(hn, gi * WIDTH_A, ga_ref, gi * WIDTH_A, tabs)
        k = normed(hn, (N_GROUPS_A + gi) * WIDTH_A, ga_ref, (N_GROUPS_A + gi) * WIDTH_A, tabs)
        v = proj(hn, (2 * N_GROUPS_A + gi) * WIDTH_A, WIDTH_A).astype(BF16)
        rows = out_ref.shape[1]
        for r in range(dilation):
            rs = slice(r * rows, (r + 1) * rows)
            out_ref[r, :, 0:WIDTH_A] = q[rs]
            out_ref[r, :, WIDTH_A:2 * WIDTH_A] = k[rs]
            out_ref[r, :, 2 * WIDTH_A:] = v[rs]

    hn = hn_for(1)
    tabs = tables_for(1)
    qb_ref[:, 0:WIDTH_B] = normed(hn, OFF_BQ, gb_ref, 0, tabs)
    t = proj(hn, OFF_BK, 2 * KV_WIDTH_B)
    kb = norm_rope(t[:, :KV_WIDTH_B], gb_ref[:, WIDTH_B:WIDTH_B + KV_WIDTH_B], tabs[0], tabs[1], tabs[2],
                   ones[:KV_WIDTH_B, :KV_WIDTH_B])
    qb_ref[:, WIDTH_B:WIDTH_B + KV_WIDTH_B] = kb.astype(BF16)
    qb_ref[:, WIDTH_B + KV_WIDTH_B:] = t[:, KV_WIDTH_B:].astype(BF16)
    g_ref[:, 0:512] = proj(hn, OFF_A_GATE, 512).astype(BF16)
    g_ref[:, 512:1024] = proj(hn, OFF_B_GATE, 512).astype(BF16)
    for j in range(2 * D_MODEL // 512):
        g_ref[:, 1024 + j * 512:1536 + j * 512] = proj(hn, OFF_MG_A + j * 512, 512).astype(BF16)


def _project(x3, nm, w, ga, gb, c_t, s1_t, s2_t, ones, tile):
    b, s, _ = x3.shape
    const = lambda bi, i: (0, 0)
    tab = lambda bi, i: (i, 0)
    row = lambda bi, i: (bi, i, 0)
    qa_specs, qa_shapes = [], []
    for _, d in DIL_PAIRS:
        qa_specs.append(pl.BlockSpec((None, d, tile // d, 3 * WIDTH_A), lambda bi, i: (bi, 0, i, 0)))
        qa_shapes.append(jax.ShapeDtypeStruct((b, d, s // d, 3 * WIDTH_A), BF16))
    return pl.pallas_call(
        _proj_kernel,
        grid=(b, s // tile),
        in_specs=[
            pl.BlockSpec((None, tile, D_MODEL), row),
            pl.BlockSpec((1, D_MODEL), const),
            pl.BlockSpec((D_MODEL, IN_COLS), const, pipeline_mode=pl.Buffered(1)),
            pl.BlockSpec((1, 2 * N_GROUPS_A * WIDTH_A), const),
            pl.BlockSpec((1, WIDTH_B + KV_WIDTH_B), const),
            pl.BlockSpec((tile, LANES), tab),
            pl.BlockSpec((tile, LANES), tab),
            pl.BlockSpec((tile, LANES), tab),
            pl.BlockSpec((SLAB, SLAB), const),
        ],
        out_specs=qa_specs + [
            pl.BlockSpec((None, tile, WIDTH_B + 2 * KV_WIDTH_B), row),
            pl.BlockSpec((None, tile, 3 * D_MODEL), row),
        ],
        out_shape=qa_shapes + [
            jax.ShapeDtypeStruct((b, s, WIDTH_B + 2 * KV_WIDTH_B), BF16),
            jax.ShapeDtypeStruct((b, s, 3 * D_MODEL), BF16),
        ],
        scratch_shapes=[pltpu.VMEM((D_MODEL // LANES, tile, LANES), F32)],
        compiler_params=pltpu.CompilerParams(
            dimension_semantics=("arbitrary", "arbitrary"), vmem_limit_bytes=VMEM_LIMIT),
        name="proj_in",
    )(x3, nm, w, ga, gb, c_t, s1_t, s2_t, ones)


def _band_bias(base, half_window, n_keys, length):
    qi = lax.broadcasted_iota(jnp.int32, (Q_BLOCK, 1), 0)
    jj = lax.broadcasted_iota(jnp.int32, (Q_BLOCK, n_keys), 1)
    lo, hi = qi, qi + 2 * half_window
    if base is not None:
        lo = jnp.maximum(lo, half_window - base)
        hi = jnp.minimum(hi, length - 1 - base + half_window)
    return jnp.where(jj >= lo, jnp.where(jj <= hi, 0.0, NEG), NEG).astype(F32)


def _block_biases(t0, n_blocks, half_window, n_keys, length):
    inner = _band_bias(None, half_window, n_keys, length) if n_blocks > 2 else None
    return [_band_bias(t0 + i * Q_BLOCK, half_window, n_keys, length) if i in (0, n_blocks - 1) else inner
            for i in range(n_blocks)]


def _window(prev_ref, main_ref, next_ref, start, n, cols):
    w = prev_ref.shape[0]
    tile = main_ref.shape[0]
    pieces = []
    if start < 0:
        pieces.append(prev_ref[w + start:w, cols])
    pieces.append(main_ref[max(start, 0):min(start + n, tile), cols])
    if start + n > tile:
        pieces.append(next_ref[0:start + n - tile, cols])
    return pieces[0] if len(pieces) == 1 else jnp.concatenate(pieces, axis=0)


def _slab_attention(q_slab, kw, vw, bias, sinks):
    head_of_lane = lax.broadcasted_iota(jnp.int32, (1, SLAB), 1) // HEAD_DIM
    zero = jnp.zeros((), BF16)
    heads = range(HEADS_PER_SLAB)
    q4 = jnp.concatenate([jnp.where(head_of_lane == h, q_slab, zero) for h in heads], axis=0)
    s4 = lax.dot_general(q4, kw, (((1,), (1,)), ((), ())), preferred_element_type=F32)
    probs, ms, ls = [], [], []
    for h in heads:
        s = s4[h * Q_BLOCK:(h + 1) * Q_BLOCK] + bias
        m = jnp.max(s, axis=-1, keepdims=True)
        if sinks is not None:
            m = jnp.maximum(m, sinks[h])
        e = jnp.exp2(s - m)
        l = jnp.sum(e, axis=-1, keepdims=True)
        if sinks is not None:
            l = l + jnp.exp2(sinks[h] - m)
        probs.append(e.astype(BF16))
        ms.append(m)
        ls.append(l)
    p4 = jnp.concatenate(probs, axis=1)
    v4 = jnp.concatenate([jnp.where(head_of_lane == h, vw, zero) for h in heads], axis=0)
    acc = jnp.dot(p4, v4, preferred_element_type=F32)
    l_slab = ls[HEADS_PER_SLAB - 1]
    for h in range(HEADS_PER_SLAB - 2, -1, -1):
        l_slab = jnp.where(head_of_lane == h, ls[h], l_slab)
    return acc * (1.0 / l_slab), ms, ls


def _attn_a_kernel(q_ref, km_ref, kp_ref, kn_ref, vm_ref, vp_ref, vn_ref, o_ref, lse_ref,
                   *, half_window, tile, length):
    w = half_window
    n_keys = Q_BLOCK + 2 * w
    n_blocks = tile // Q_BLOCK
    biases = _block_biases(pl.program_id(2) * tile, n_blocks, w, n_keys, length)
    lane = lax.broadcasted_iota(jnp.int32, (1, LANES), 1)
    for i in range(n_blocks):
        r0 = i * Q_BLOCK
        m_rows = jnp.zeros((Q_BLOCK, LANES), F32)
        l_rows = jnp.ones((Q_BLOCK, LANES), F32)
        for c in range(WIDTH_A // SLAB):
            cols = slice(c * SLAB, (c + 1) * SLAB)
            out, ms, ls = _slab_attention(q_ref[r0:r0 + Q_BLOCK, cols],
                                          _window(kp_ref, km_ref, kn_ref, r0 - w, n_keys, cols),
                                          _window(vp_ref, vm_ref, vn_ref, r0 - w, n_keys, cols),
                                          biases[i], None)
            o_ref[r0:r0 + Q_BLOCK, cols] = out.astype(BF16)
            for h in range(HEADS_PER_SLAB):
                m_rows = jnp.where(lane == c * HEADS_PER_SLAB + h, ms[h], m_rows)
                l_rows = jnp.where(lane == c * HEADS_PER_SLAB + h, ls[h], l_rows)
        lse_ref[r0:r0 + Q_BLOCK, :] = m_rows + jnp.log2(l_rows)


def _attention_a(qkv, window, tile):
    b, dilation, length, _ = qkv.shape
    half_window = (window // 2) // dilation
    tile = min(tile, length)
    halo_per_tile = tile // half_window
    n_halo = length // half_window

    def main(sec):
        return pl.BlockSpec((None, None, tile, WIDTH_A), lambda bi, r, l: (bi, r, l, sec))

    def prev(sec):
        return pl.BlockSpec((None, None, half_window, WIDTH_A),
                            lambda bi, r, l: (bi, r, jnp.maximum(l * halo_per_tile - 1, 0), sec))

    def nxt(sec):
        return pl.BlockSpec((None, None, half_window, WIDTH_A),
                            lambda bi, r, l: (bi, r, jnp.minimum((l + 1) * halo_per_tile, n_halo - 1), sec))

    return pl.pallas_call(
        functools.partial(_attn_a_kernel, half_window=half_window, tile=tile, length=length),
        grid=(b, dilation, length // tile),
        in_specs=[main(0), main(1), prev(1), nxt(1), main(2), prev(2), nxt(2)],
        out_specs=[
            pl.BlockSpec((None, None, tile, WIDTH_A), lambda bi, r, l: (bi, r, l, 0)),
            pl.BlockSpec((None, None, tile, LANES), lambda bi, r, l: (bi, r, l, 0)),
        ],
        out_shape=[
            jax.ShapeDtypeStruct((b, dilation, length, WIDTH_A), BF16),
            jax.ShapeDtypeStruct((b, dilation, length, LANES), F32),
        ],
        compiler_params=pltpu.CompilerParams(
            dimension_semantics=("arbitrary", "arbitrary", "arbitrary"),
            vmem_limit_bytes=VMEM_LIMIT),
        name=f"attn_a_d{dilation}",
    )(qkv, qkv, qkv, qkv, qkv, qkv, qkv)


def _attn_b_kernel(sink_ref, q_ref, km_ref, kp_ref, kn_ref, vm_ref, vp_ref, vn_ref, o_ref,
                   *, tile, length):
    w = WINDOW_B
    n_keys = Q_BLOCK + 2 * w
    n_blocks = tile // Q_BLOCK
    biases = _block_biases(pl.program_id(1) * tile, n_blocks, w, n_keys, length)
    even_block = (lax.broadcasted_iota(jnp.int32, (1, SLAB), 1) // HEAD_DIM) % 2 == 0
    all_lanes = slice(0, KV_WIDTH_B)

    def per_kv_head(x):
        swapped = jnp.concatenate([x[:, HEAD_DIM:], x[:, :HEAD_DIM]], axis=1)
        pair = jnp.concatenate([x, x], axis=1)
        riap = jnp.concatenate([swapped, swapped], axis=1)
        return jnp.where(even_block, pair, riap), jnp.where(even_block, riap, pair)

    for i in range(n_blocks):
        r0 = i * Q_BLOCK
        k_heads = per_kv_head(_window(kp_ref, km_ref, kn_ref, r0 - w, n_keys, all_lanes))
        v_heads = per_kv_head(_window(vp_ref, vm_ref, vn_ref, r0 - w, n_keys, all_lanes))
        for c in range(WIDTH_B // SLAB):
            sinks = [sink_ref[c * HEADS_PER_SLAB + h] * LOG2E for h in range(HEADS_PER_SLAB)]
            cols = slice(c * SLAB, (c + 1) * SLAB)
            out, _, _ = _slab_attention(q_ref[r0:r0 + Q_BLOCK, cols], k_heads[c], v_heads[c],
                                        biases[i], sinks)
            o_ref[r0:r0 + Q_BLOCK, cols] = out.astype(BF16)


def _attention_b(qb3, sink, tile):
    b, s, _ = qb3.shape
    tile = min(tile, s)
    halo_per_tile = tile // WINDOW_B
    n_halo = s // WINDOW_B
    k_col = WIDTH_B // KV_WIDTH_B
    v_col = k_col + 1

    def main(col):
        return pl.BlockSpec((None, tile, KV_WIDTH_B), lambda bi, l, sk: (bi, l, col))

    def prev(col):
        return pl.BlockSpec((None, WINDOW_B, KV_WIDTH_B),
                            lambda bi, l, sk: (bi, jnp.maximum(l * halo_per_tile - 1, 0), col))

    def nxt(col):
        return pl.BlockSpec((None, WINDOW_B, KV_WIDTH_B),
                            lambda bi, l, sk: (bi, jnp.minimum((l + 1) * halo_per_tile, n_halo - 1), col))

    return pl.pallas_call(
        functools.partial(_attn_b_kernel, tile=tile, length=s),
        grid_spec=pltpu.PrefetchScalarGridSpec(
            num_scalar_prefetch=1,
            grid=(b, s // tile),
            in_specs=[
                pl.BlockSpec((None, tile, WIDTH_B), lambda bi, l, sk: (bi, l, 0)),
                main(k_col), prev(k_col), nxt(k_col), main(v_col), prev(v_col), nxt(v_col),
            ],
            out_specs=pl.BlockSpec((None, tile, WIDTH_B), lambda bi, l, sk: (bi, l, 0)),
        ),
        out_shape=jax.ShapeDtypeStruct((b, s, WIDTH_B), BF16),
        compiler_params=pltpu.CompilerParams(
            dimension_semantics=("arbitrary", "arbitrary"), vmem_limit_bytes=VMEM_LIMIT),
        name="attn_b",
    )(sink, qb3, qb3, qb3, qb3, qb3, qb3, qb3)


def _sigmoid(x):
    half = jnp.asarray(0.5, x.dtype)
    return half * jnp.tanh(half * x) + half


def _merge_kernel(x_ref, p_ref, o0_ref, o1_ref, o2_ref, l0_ref, l1_ref, l2_ref, bo_ref, g_ref,
                  wa_ref, wb_ref, wo_ref, wg_ref, wp_ref, np_ref, ex_ref, y_ref, oslab, lslab):
    n_slabs = WIDTH_A // LANES

    def token_major(o_ref, l_ref, dilation, slot):
        if dilation == 1:
            return o_ref[0].astype(F32), l_ref[0]
        rows = o_ref.shape[1]
        for r in range(dilation):
            o_r = o_ref[r].astype(F32)
            for j in range(n_slabs):
                oslab[slot, j, pl.ds(r, rows, stride=dilation), :] = o_r[:, j * LANES:(j + 1) * LANES]
            lslab[slot, pl.ds(r, rows, stride=dilation), :] = l_ref[r]
        return jnp.concatenate([oslab[slot, j] for j in range(n_slabs)], axis=1), lslab[slot]

    groups = [token_major(o_ref, l_ref, d, slot) for slot, (o_ref, l_ref, (_, d)) in
              enumerate(zip((o0_ref, o1_ref, o2_ref), (l0_ref, l1_ref, l2_ref), DIL_PAIRS))]
    l0, l1, l2 = (g[1] for g in groups)
    mx = jnp.maximum(jnp.maximum(l0, l1), l2)
    e0 = jnp.exp2(l0 - mx)
    e1 = jnp.exp2(l1 - mx)
    e2 = jnp.exp2(l2 - mx)
    inv = 1.0 / (e0 + e1 + e2)
    ex = ex_ref[...]

    def spread(wt):
        return jnp.dot(wt.astype(BF16), ex, preferred_element_type=F32)

    a_out = (spread(e0 * inv) * groups[0][0] + spread(e1 * inv) * groups[1][0]
             + spread(e2 * inv) * groups[2][0])
    a_gate = g_ref[:, 0:512]
    b_gate = g_ref[:, 512:1024]
    a_in = (a_out * (a_gate * _sigmoid(a_gate)).astype(F32)).astype(BF16)
    b_in = bo_ref[...] * (b_gate * _sigmoid(b_gate))
    a_y = jnp.dot(a_in, wa_ref[...], preferred_element_type=F32)
    b_y = jnp.dot(b_in, wb_ref[...], preferred_element_type=F32)
    merged = (_sigmoid(g_ref[:, 1024:2048]).astype(F32) * a_y
              + _sigmoid(g_ref[:, 2048:3072]).astype(F32) * b_y)
    x1 = x_ref[...] + jnp.dot(merged.astype(BF16), wo_ref[...], preferred_element_type=F32)
    ms = jnp.mean(x1 * x1, axis=-1, keepdims=True)
    xn = (x1 * lax.rsqrt(ms + EPS) * np_ref[...]).astype(BF16)
    gate = _sigmoid(jnp.dot(xn, wg_ref[...], preferred_element_type=F32))
    ple = jnp.dot(p_ref[...].astype(BF16), wp_ref[...], preferred_element_type=F32)
    y_ref[...] = x1 + gate * ple


def _merge(x3, p3, outs, bo, g, wa, wb, wo, wg, wp, npl, ex, tile):
    b, s, _ = x3.shape
    row = lambda bi, i: (bi, i, 0)
    const = lambda bi, i: (0, 0)

    def rows(width):
        return pl.BlockSpec((None, tile, width), row)

    def grouped(width, d):
        return pl.BlockSpec((None, d, tile // d, width), lambda bi, i: (bi, 0, i, 0))

    def whole(r, c):
        return pl.BlockSpec((r, c), const, pipeline_mode=pl.Buffered(1))

    dils = [d for _, d in DIL_PAIRS]
    return pl.pallas_call(
        _merge_kernel,
        grid=(b, s // tile),
        in_specs=[rows(D_MODEL), rows(PLE_DIM)]
        + [grouped(WIDTH_A, d) for d in dils] + [grouped(LANES, d) for d in dils]
        + [rows(WIDTH_B), rows(3 * D_MODEL),
           whole(WIDTH_A, D_MODEL), whole(WIDTH_B, D_MODEL), whole(D_MODEL, D_MODEL),
           whole(D_MODEL, D_MODEL), whole(PLE_DIM, D_MODEL), whole(1, D_MODEL),
           whole(LANES, WIDTH_A)],
        out_specs=rows(D_MODEL),
        out_shape=jax.ShapeDtypeStruct((b, s, D_MODEL), F32),
        scratch_shapes=[
            pltpu.VMEM((N_GROUPS_A, WIDTH_A // LANES, tile, LANES), F32),
            pltpu.VMEM((N_GROUPS_A, tile, LANES), F32),
        ],
        compiler_params=pltpu.CompilerParams(
            dimension_semantics=("arbitrary", "arbitrary"), vmem_limit_bytes=VMEM_LIMIT),
        name="merge_out",
    )(x3, p3, *[o for o, _ in outs], *[l for _, l in outs], bo, g, wa, wb, wo, wg, wp, npl, ex)


def _rope_tables(s):
    inv = ROPE_THETA ** (-jnp.arange(0, ROT_DIM, 2, dtype=F32) / ROT_DIM)
    ang = jnp.arange(s, dtype=F32)[:, None] * inv[None, :]
    ang = jnp.concatenate([ang, ang], axis=-1)
    cos = jnp.cos(ang)
    sin = jnp.sin(ang)
    half = ROT_DIM // 2
    pad = HEAD_DIM - ROT_DIM
    c_head = jnp.concatenate([cos, jnp.ones((s, pad), F32)], axis=1)
    s1_head = jnp.concatenate([-sin[:, :half], jnp.zeros((s, HEAD_DIM - half), F32)], axis=1)
    s2_head = jnp.concatenate([jnp.zeros((s, half), F32), sin[:, half:], jnp.zeros((s, pad), F32)], axis=1)
    reps = LANES // HEAD_DIM
    return jnp.tile(c_head, (1, reps)), jnp.tile(s1_head, (1, reps)), jnp.tile(s2_head, (1, reps))


def _layer_constants():
    lane = jnp.arange(SLAB)
    ones = ((lane[:, None] // HEAD_DIM == lane[None, :] // HEAD_DIM) * (1.0 / HEAD_DIM)).astype(BF16)
    spread = (jnp.arange(LANES)[:, None] == jnp.arange(WIDTH_A)[None, :] // HEAD_DIM).astype(BF16)
    return ones, spread


def _trunk(x, p, norm_mix, w_in, a_q_norm, a_k_norm, b_q_norm, b_k_norm, b_sink, w_branch_a,
           w_branch_b, w_out, norm_ple, w_ple, w_ple_gate):
    b, s, _ = x.shape
    depth = w_in.shape[0]
    row_tile = min(ROW_TILE, s)
    c_t, s1_t, s2_t = _rope_tables(s)
    ones, spread = _layer_constants()
    scale = HEAD_DIM ** -0.5 * LOG2E
    heads = WIDTH_A // HEAD_DIM
    for i in range(depth):
        ga = jnp.concatenate([jnp.tile(a_q_norm[i] * scale, (1, heads)).reshape(1, -1),
                              jnp.tile(a_k_norm[i], (1, heads)).reshape(1, -1)], axis=1)
        gb = jnp.concatenate([jnp.tile(b_q_norm[i] * scale, N_HEADS_B),
                              jnp.tile(b_k_norm[i], KV_WIDTH_B // HEAD_DIM)])[None, :]
        qa0, qa1, qa2, qb, g = _project(x, norm_mix[i][None, :], w_in[i].astype(BF16), ga, gb,
                                        c_t, s1_t, s2_t, ones, row_tile)
        outs = [_attention_a(qkv, window, ATTN_TILE)
                for qkv, (window, _) in zip((qa0, qa1, qa2), DIL_PAIRS)]
        bo = _attention_b(qb, b_sink[i], ATTN_TILE)
        x = _merge(x, p[i], outs, bo, g,
                   w_branch_a[i].astype(BF16), w_branch_b[i].astype(BF16), w_out[i].astype(BF16),
                   w_ple_gate[i].astype(BF16), w_ple[i].astype(BF16), norm_ple[i][None, :],
                   spread, row_tile)
    return x


def kernel(x_prompt, x_sample, p_prompt, p_sample, norm_mix, w_in, a_q_norm, a_k_norm, b_q_norm,
           b_k_norm, b_sink, w_branch_a, w_branch_b, w_out, norm_ple, w_ple, w_ple_gate):
    weights = (norm_mix, w_in, a_q_norm, a_k_norm, b_q_norm, b_k_norm, b_sink, w_branch_a,
               w_branch_b, w_out, norm_ple, w_ple, w_ple_gate)
    return (_trunk(x_prompt, p_prompt, *weights), _trunk(x_sample, p_sample, *weights))
```

```python
import functools

import jax
import jax.numpy as jnp
from jax import lax
from jax.experimental import pallas as pl
from jax.experimental.pallas import tpu as pltpu

D_MODEL = 1024
HEAD_DIM = 64
ROT_DIM = HEAD_DIM // 4
ROPE_THETA = 500000.0
DIL_PAIRS = ((128, 1), (512, 4), (2048, 16))
N_GROUPS_A = 3
WIDTH_A = 512
WIDTH_B = 512
KV_WIDTH_B = 128
N_HEADS_B = 8
WINDOW_B = 128
PLE_DIM = 256
A_QKV_COLS = 3 * N_GROUPS_A * WIDTH_A
IN_COLS = A_QKV_COLS + WIDTH_A + WIDTH_B + 2 * KV_WIDTH_B + WIDTH_B + 2 * D_MODEL
EPS = 1e-6
NEG = -1e30
LOG2E = 1.4426950408889634

OFF_A_GATE = A_QKV_COLS
OFF_BQ = OFF_A_GATE + WIDTH_A
OFF_BK = OFF_BQ + WIDTH_B
OFF_BV = OFF_BK + KV_WIDTH_B
OFF_B_GATE = OFF_BV + KV_WIDTH_B
OFF_MG_A = OFF_B_GATE + WIDTH_B
OFF_MG_B = OFF_MG_A + D_MODEL

LANES = 128
SLAB = 256
HEADS_PER_SLAB = SLAB // HEAD_DIM
Q_BLOCK = 128
ROW_TILE = 512
ATTN_TILE = 1024
VMEM_LIMIT = 56 * 1024 * 1024

BF16 = jnp.bfloat16
F32 = jnp.float32


def _residue_major(ref, dilation):
    rows = ref.shape[0] // dilation
    return jnp.concatenate([ref[pl.ds(r, rows, stride=dilation), :] for r in range(dilation)], axis=0)


def _proj_kernel(x_ref, nm_ref, w_ref, ga_ref, gb_ref, c_ref, s1_ref, s2_ref, ones_ref,
                 qa0_ref, qa1_ref, qa2_ref, qb_ref, g_ref, hslab):
    xf = x_ref[...]
    ms = jnp.mean(xf * xf, axis=-1, keepdims=True)
    hn32 = xf * lax.rsqrt(ms + EPS) * nm_ref[...]
    n_slabs = D_MODEL // LANES
    for j in range(n_slabs):
        hslab[j] = hn32[:, j * LANES:(j + 1) * LANES]
    ones = ones_ref[...]

    def hn_for(dilation):
        if dilation == 1:
            return hn32.astype(BF16)
        return jnp.concatenate([_residue_major(hslab.at[j], dilation) for j in range(n_slabs)],
                               axis=1).astype(BF16)

    def tables_for(dilation):
        if dilation == 1:
            return c_ref[...], s1_ref[...], s2_ref[...]
        return tuple(_residue_major(t, dilation) for t in (c_ref, s1_ref, s2_ref))

    def wide(t):
        return jnp.concatenate([t, t], axis=1)

    def norm_rope(t, gain, cc, sa, sb, ones_m):
        width = t.shape[1]
        ms = jnp.dot((t * t).astype(BF16), ones_m, preferred_element_type=F32)
        n = t * lax.rsqrt(ms + EPS) * gain
        up = pltpu.roll(n, width - ROT_DIM // 2, 1)
        dn = pltpu.roll(n, ROT_DIM // 2, 1)
        return n * cc + up * sa + dn * sb

    def proj(hn, lo, width):
        return jnp.dot(hn, w_ref[:, lo:lo + width], preferred_element_type=F32)

    def normed(hn, lo, gains, glo, tabs):
        t = proj(hn, lo, WIDTH_A)
        cc, sa, sb = (wide(t_) for t_ in tabs)
        halves = [norm_rope(t[:, h * SLAB:(h + 1) * SLAB], gains[:, glo + h * SLAB:glo + (h + 1) * SLAB],
                            cc, sa, sb, ones) for h in range(WIDTH_A // SLAB)]
        return jnp.concatenate(halves, axis=1).astype(BF16)

    for gi, (out_ref, (_, dilation)) in enumerate(zip((qa0_ref, qa1_ref, qa2_ref), DIL_PAIRS)):
        hn = hn_for(dilation)
        tabs = tables_for(dilation)
        q = normed(hn, gi * WIDTH_A, ga_ref, gi * WIDTH_A, tabs)
        k = normed(hn, (N_GROUPS_A + gi) * WIDTH_A, ga_ref, (N_GROUPS_A + gi) * WIDTH_A, tabs)
        v = proj(hn, (2 * N_GROUPS_A + gi) * WIDTH_A, WIDTH_A).astype(BF16)
        rows = out_ref.shape[1]
        for r in range(dilation):
            rs = slice(r * rows, (r + 1) * rows)
            out_ref[r, :, 0:WIDTH_A] = q[rs]
            out_ref[r, :, WIDTH_A:2 * WIDTH_A] = k[rs]
            out_ref[r, :, 2 * WIDTH_A:] = v[rs]

    hn = hn_for(1)
    tabs = tables_for(1)
    qb_ref[:, 0:WIDTH_B] = normed(hn, OFF_BQ, gb_ref, 0, tabs)
    t = proj(hn, OFF_BK, 2 * KV_WIDTH_B)
    kb = norm_rope(t[:, :KV_WIDTH_B], gb_ref[:, WIDTH_B:WIDTH_B + KV_WIDTH_B], tabs[0], tabs[1], tabs[2],
                   ones[:KV_WIDTH_B, :KV_WIDTH_B])
    qb_ref[:, WIDTH_B:WIDTH_B + KV_WIDTH_B] = kb.astype(BF16)
    qb_ref[:, WIDTH_B + KV_WIDTH_B:] = t[:, KV_WIDTH_B:].astype(BF16)
    g_ref[:, 0:512] = proj(hn, OFF_A_GATE, 512).astype(BF16)
    g_ref[:, 512:1024] = proj(hn, OFF_B_GATE, 512).astype(BF16)
    for j in range(2 * D_MODEL // 512):
        g_ref[:, 1024 + j * 512:1536 + j * 512] = proj(hn, OFF_MG_A + j * 512, 512).astype(BF16)


def _project(x3, nm, w, ga, gb, c_t, s1_t, s2_t, ones, tile):
    b, s, _ = x3.shape
    const = lambda bi, i: (0, 0)
    tab = lambda bi, i: (i, 0)
    row = lambda bi, i: (bi, i, 0)
    qa_specs, qa_shapes = [], []
    for _, d in DIL_PAIRS:
        qa_specs.append(pl.BlockSpec((None, d, tile // d, 3 * WIDTH_A), lambda bi, i: (bi, 0, i, 0)))
        qa_shapes.append(jax.ShapeDtypeStruct((b, d, s // d, 3 * WIDTH_A), BF16))
    return pl.pallas_call(
        _proj_kernel,
        grid=(b, s // tile),
        in_specs=[
            pl.BlockSpec((None, tile, D_MODEL), row),
            pl.BlockSpec((1, D_MODEL), const),
            pl.BlockSpec((D_MODEL, IN_COLS), const, pipeline_mode=pl.Buffered(1)),
            pl.BlockSpec((1, 2 * N_GROUPS_A * WIDTH_A), const),
            pl.BlockSpec((1, WIDTH_B + KV_WIDTH_B), const),
            pl.BlockSpec((tile, LANES), tab),
            pl.BlockSpec((tile, LANES), tab),
            pl.BlockSpec((tile, LANES), tab),
            pl.BlockSpec((SLAB, SLAB), const),
        ],
        out_specs=qa_specs + [
            pl.BlockSpec((None, tile, WIDTH_B + 2 * KV_WIDTH_B), row),
            pl.BlockSpec((None, tile, 3 * D_MODEL), row),
        ],
        out_shape=qa_shapes + [
            jax.ShapeDtypeStruct((b, s, WIDTH_B + 2 * KV_WIDTH_B), BF16),
            jax.ShapeDtypeStruct((b, s, 3 * D_MODEL), BF16),
        ],
        scratch_shapes=[pltpu.VMEM((D_MODEL // LANES, tile, LANES), F32)],
        compiler_params=pltpu.CompilerParams(
            dimension_semantics=("arbitrary", "arbitrary"), vmem_limit_bytes=VMEM_LIMIT),
        name="proj_in",
    )(x3, nm, w, ga, gb, c_t, s1_t, s2_t, ones)


def _band_bias(base, half_window, n_keys, length):
    qi = lax.broadcasted_iota(jnp.int32, (Q_BLOCK, 1), 0)
    jj = lax.broadcasted_iota(jnp.int32, (Q_BLOCK, n_keys), 1)
    lo, hi = qi, qi + 2 * half_window
    if base is not None:
        lo = jnp.maximum(lo, half_window - base)
        hi = jnp.minimum(hi, length - 1 - base + half_window)
    return jnp.where(jj >= lo, jnp.where(jj <= hi, 0.0, NEG), NEG).astype(F32)


def _block_biases(t0, n_blocks, half_window, n_keys, length):
    inner = _band_bias(None, half_window, n_keys, length) if n_blocks > 2 else None
    return [_band_bias(t0 + i * Q_BLOCK, half_window, n_keys, length) if i in (0, n_blocks - 1) else inner
            for i in range(n_blocks)]


def _window(prev_ref, main_ref, next_ref, start, n, cols):
    w = prev_ref.shape[0]
    tile = main_ref.shape[0]
    pieces = []
    if start < 0:
        pieces.append(prev_ref[w + start:w, cols])
    pieces.append(main_ref[max(start, 0):min(start + n, tile), cols])
    if start + n > tile:
        pieces.append(next_ref[0:start + n - tile, cols])
    return pieces[0] if len(pieces) == 1 else jnp.concatenate(pieces, axis=0)


def _slab_attention(q_slab, kw, vw, bias, sinks):
    head_of_lane = lax.broadcasted_iota(jnp.int32, (1, SLAB), 1) // HEAD_DIM
    heads = range(HEADS_PER_SLAB)
    q_bits = pltpu.bitcast(q_slab, jnp.uint32)
    keep = [jnp.where(head_of_lane == h, jnp.uint32(0xFFFFFFFF), jnp.uint32(0)) for h in heads]
    q4 = jnp.concatenate([pltpu.bitcast(q_bits & keep[h], BF16) for h in heads], axis=0)
    s4 = lax.dot_general(q4, kw, (((1,), (1,)), ((), ())), preferred_element_type=F32)
    probs, ms, ls = [], [], []
    for h in heads:
        s = s4[h * Q_BLOCK:(h + 1) * Q_BLOCK] + bias
        m = jnp.max(s, axis=-1, keepdims=True)
        if sinks is not None:
            m = jnp.maximum(m, sinks[h])
        e = jnp.exp2(s - m)
        l = jnp.sum(e, axis=-1, keepdims=True)
        if sinks is not None:
            l = l + jnp.exp2(sinks[h] - m)
        probs.append(e.astype(BF16))
        ms.append(m)
        ls.append(l)
    acc4 = jnp.dot(jnp.concatenate(probs, axis=0), vw, preferred_element_type=F32)
    out = None
    for h in reversed(heads):
        part = acc4[h * Q_BLOCK:(h + 1) * Q_BLOCK] * (1.0 / ls[h])
        out = part if out is None else jnp.where(head_of_lane == h, part, out)
    return out, ms, ls


def _attn_a_kernel(q_ref, km_ref, kp_ref, kn_ref, vm_ref, vp_ref, vn_ref, o_ref, lse_ref,
                   *, half_window, tile, length):
    w = half_window
    n_keys = Q_BLOCK + 2 * w
    n_blocks = tile // Q_BLOCK
    biases = _block_biases(pl.program_id(2) * tile, n_blocks, w, n_keys, length)
    lane = lax.broadcasted_iota(jnp.int32, (1, LANES), 1)
    for i in range(n_blocks):
        r0 = i * Q_BLOCK
        m_rows = jnp.zeros((Q_BLOCK, LANES), F32)
        l_rows = jnp.ones((Q_BLOCK, LANES), F32)
        for c in range(WIDTH_A // SLAB):
            cols = slice(c * SLAB, (c + 1) * SLAB)
            out, ms, ls = _slab_attention(q_ref[r0:r0 + Q_BLOCK, cols],
                                          _window(kp_ref, km_ref, kn_ref, r0 - w, n_keys, cols),
                                          _window(vp_ref, vm_ref, vn_ref, r0 - w, n_keys, cols),
                                          biases[i], None)
            o_ref[r0:r0 + Q_BLOCK, cols] = out.astype(BF16)
            for h in range(HEADS_PER_SLAB):
                m_rows = jnp.where(lane == c * HEADS_PER_SLAB + h, ms[h], m_rows)
                l_rows = jnp.where(lane == c * HEADS_PER_SLAB + h, ls[h], l_rows)
        lse_ref[r0:r0 + Q_BLOCK, :] = m_rows + jnp.log2(l_rows)


def _attention_a(qkv, window, tile):
    b, dilation, length, _ = qkv.shape
    half_window = (window // 2) // dilation
    tile = min(tile, length)
    halo_per_tile = tile // half_window
    n_halo = length // half_window

    def main(sec):
        return pl.BlockSpec((None, None, tile, WIDTH_A), lambda bi, r, l: (bi, r, l, sec))

    def prev(sec):
        return pl.BlockSpec((None, None, half_window, WIDTH_A),
                            lambda bi, r, l: (bi, r, jnp.maximum(l * halo_per_tile - 1, 0), sec))

    def nxt(sec):
        return pl.BlockSpec((None, None, half_window, WIDTH_A),
                            lambda bi, r, l: (bi, r, jnp.minimum((l + 1) * halo_per_tile, n_halo - 1), sec))

    return pl.pallas_call(
        functools.partial(_attn_a_kernel, half_window=half_window, tile=tile, length=length),
        grid=(b, dilation, length // tile),
        in_specs=[main(0), main(1), prev(1), nxt(1), main(2), prev(2), nxt(2)],
        out_specs=[
            pl.BlockSpec((None, None, tile, WIDTH_A), lambda bi, r, l: (bi, r, l, 0)),
            pl.BlockSpec((None, None, tile, LANES), lambda bi, r, l: (bi, r, l, 0)),
        ],
        out_shape=[
            jax.ShapeDtypeStruct((b, dilation, length, WIDTH_A), BF16),
            jax.ShapeDtypeStruct((b, dilation, length, LANES), F32),
        ],
        compiler_params=pltpu.CompilerParams(
            dimension_semantics=("arbitrary", "arbitrary", "arbitrary"),
            vmem_limit_bytes=VMEM_LIMIT),
        name=f"attn_a_d{dilation}",
    )(qkv, qkv, qkv, qkv, qkv, qkv, qkv)


def _attn_b_kernel(sink_ref, q_ref, km_ref, kp_ref, kn_ref, vm_ref, vp_ref, vn_ref, o_ref,
                   *, tile, length):
    w = WINDOW_B
    n_keys = Q_BLOCK + 2 * w
    n_blocks = tile // Q_BLOCK
    biases = _block_biases(pl.program_id(1) * tile, n_blocks, w, n_keys, length)
    even_block = (lax.broadcasted_iota(jnp.int32, (1, SLAB), 1) // HEAD_DIM) % 2 == 0
    all_lanes = slice(0, KV_WIDTH_B)

    def per_kv_head(x):
        swapped = jnp.concatenate([x[:, HEAD_DIM:], x[:, :HEAD_DIM]], axis=1)
        pair = jnp.concatenate([x, x], axis=1)
        riap = jnp.concatenate([swapped, swapped], axis=1)
        return jnp.where(even_block, pair, riap), jnp.where(even_block, riap, pair)

    for i in range(n_blocks):
        r0 = i * Q_BLOCK
        k_heads = per_kv_head(_window(kp_ref, km_ref, kn_ref, r0 - w, n_keys, all_lanes))
        v_heads = per_kv_head(_window(vp_ref, vm_ref, vn_ref, r0 - w, n_keys, all_lanes))
        for c in range(WIDTH_B // SLAB):
            sinks = [sink_ref[c * HEADS_PER_SLAB + h] * LOG2E for h in range(HEADS_PER_SLAB)]
            cols = slice(c * SLAB, (c + 1) * SLAB)
            out, _, _ = _slab_attention(q_ref[r0:r0 + Q_BLOCK, cols], k_heads[c], v_heads[c],
                                        biases[i], sinks)
            o_ref[r0:r0 + Q_BLOCK, cols] = out.astype(BF16)


def _attention_b(qb3, sink, tile):
    b, s, _ = qb3.shape
    tile = min(tile, s)
    halo_per_tile = tile // WINDOW_B
    n_halo = s // WINDOW_B
    k_col = WIDTH_B // KV_WIDTH_B
    v_col = k_col + 1

    def main(col):
        return pl.BlockSpec((None, tile, KV_WIDTH_B), lambda bi, l, sk: (bi, l, col))

    def prev(col):
        return pl.BlockSpec((None, WINDOW_B, KV_WIDTH_B),
                            lambda bi, l, sk: (bi, jnp.maximum(l * halo_per_tile - 1, 0), col))

    def nxt(col):
        return pl.BlockSpec((None, WINDOW_B, KV_WIDTH_B),
                            lambda bi, l, sk: (bi, jnp.minimum((l + 1) * halo_per_tile, n_halo - 1), col))

    return pl.pallas_call(
        functools.partial(_attn_b_kernel, tile=tile, length=s),
        grid_spec=pltpu.PrefetchScalarGridSpec(
            num_scalar_prefetch=1,
            grid=(b, s // tile),
            in_specs=[
                pl.BlockSpec((None, tile, WIDTH_B), lambda bi, l, sk: (bi, l, 0)),
                main(k_col), prev(k_col), nxt(k_col), main(v_col), prev(v_col), nxt(v_col),
            ],
            out_specs=pl.BlockSpec((None, tile, WIDTH_B), lambda bi, l, sk: (bi, l, 0)),
        ),
        out_shape=jax.ShapeDtypeStruct((b, s, WIDTH_B), BF16),
        compiler_params=pltpu.CompilerParams(
            dimension_semantics=("arbitrary", "arbitrary"), vmem_limit_bytes=VMEM_LIMIT),
        name="attn_b",
    )(sink, qb3, qb3, qb3, qb3, qb3, qb3, qb3)


def _sigmoid(x):
    half = jnp.asarray(0.5, x.dtype)
    return half * jnp.tanh(half * x) + half


def _merge_kernel(x_ref, p_ref, o0_ref, o1_ref, o2_ref, l0_ref, l1_ref, l2_ref, bo_ref, g_ref,
                  wa_ref, wb_ref, wo_ref, wg_ref, wp_ref, np_ref, ex_ref, y_ref, oslab, lslab):
    n_slabs = WIDTH_A // LANES

    def token_major(o_ref, l_ref, dilation, slot):
        if dilation == 1:
            return o_ref[0].astype(F32), l_ref[0]
        rows = o_ref.shape[1]
        for r in range(dilation):
            o_r = o_ref[r].astype(F32)
            for j in range(n_slabs):
                oslab[slot, j, pl.ds(r, rows, stride=dilation), :] = o_r[:, j * LANES:(j + 1) * LANES]
            lslab[slot, pl.ds(r, rows, stride=dilation), :] = l_ref[r]
        return jnp.concatenate([oslab[slot, j] for j in range(n_slabs)], axis=1), lslab[slot]

    groups = [token_major(o_ref, l_ref, d, slot) for slot, (o_ref, l_ref, (_, d)) in
              enumerate(zip((o0_ref, o1_ref, o2_ref), (l0_ref, l1_ref, l2_ref), DIL_PAIRS))]
    l0, l1, l2 = (g[1] for g in groups)
    mx = jnp.maximum(jnp.maximum(l0, l1), l2)
    e0 = jnp.exp2(l0 - mx)
    e1 = jnp.exp2(l1 - mx)
    e2 = jnp.exp2(l2 - mx)
    inv = 1.0 / (e0 + e1 + e2)
    ex = ex_ref[...]

    def spread(wt):
        return jnp.dot(wt.astype(BF16), ex, preferred_element_type=F32)

    a_out = (spread(e0 * inv) * groups[0][0] + spread(e1 * inv) * groups[1][0]
             + spread(e2 * inv) * groups[2][0])
    a_gate = g_ref[:, 0:512]
    b_gate = g_ref[:, 512:1024]
    a_in = (a_out * (a_gate * _sigmoid(a_gate)).astype(F32)).astype(BF16)
    b_in = bo_ref[...] * (b_gate * _sigmoid(b_gate))
    a_y = jnp.dot(a_in, wa_ref[...], preferred_element_type=F32)
    b_y = jnp.dot(b_in, wb_ref[...], preferred_element_type=F32)
    merged = (_sigmoid(g_ref[:, 1024:2048]).astype(F32) * a_y
              + _sigmoid(g_ref[:, 2048:3072]).astype(F32) * b_y)
    x1 = x_ref[...] + jnp.dot(merged.astype(BF16), wo_ref[...], preferred_element_type=F32)
    ms = jnp.mean(x1 * x1, axis=-1, keepdims=True)
    xn = (x1 * lax.rsqrt(ms + EPS) * np_ref[...]).astype(BF16)
    gate = _sigmoid(jnp.dot(xn, wg_ref[...], preferred_element_type=F32))
    ple = jnp.dot(p_ref[...].astype(BF16), wp_ref[...], preferred_element_type=F32)
    y_ref[...] = x1 + gate * ple


def _merge(x3, p4, layer, outs, bo, g, wa, wb, wo, wg, wp, npl, ex, tile):
    b, s, _ = x3.shape
    row = lambda bi, i: (bi, i, 0)
    const = lambda bi, i: (0, 0)

    def rows(width):
        return pl.BlockSpec((None, tile, width), row)

    def grouped(width, d):
        return pl.BlockSpec((None, d, tile // d, width), lambda bi, i: (bi, 0, i, 0))

    def whole(r, c):
        return pl.BlockSpec((r, c), const, pipeline_mode=pl.Buffered(1))

    dils = [d for _, d in DIL_PAIRS]
    return pl.pallas_call(
        _merge_kernel,
        grid=(b, s // tile),
        in_specs=[rows(D_MODEL), pl.BlockSpec((None, None, tile, PLE_DIM), lambda bi, i: (layer, bi, i, 0))]
        + [grouped(WIDTH_A, d) for d in dils] + [grouped(LANES, d) for d in dils]
        + [rows(WIDTH_B), rows(3 * D_MODEL),
           whole(WIDTH_A, D_MODEL), whole(WIDTH_B, D_MODEL), whole(D_MODEL, D_MODEL),
           whole(D_MODEL, D_MODEL), whole(PLE_DIM, D_MODEL), whole(1, D_MODEL),
           whole(LANES, WIDTH_A)],
        out_specs=rows(D_MODEL),
        out_shape=jax.ShapeDtypeStruct((b, s, D_MODEL), F32),
        scratch_shapes=[
            pltpu.VMEM((N_GROUPS_A, WIDTH_A // LANES, tile, LANES), F32),
            pltpu.VMEM((N_GROUPS_A, tile, LANES), F32),
        ],
        compiler_params=pltpu.CompilerParams(
            dimension_semantics=("arbitrary", "arbitrary"), vmem_limit_bytes=VMEM_LIMIT),
        name="merge_out",
    )(x3, p4, *[o for o, _ in outs], *[l for _, l in outs], bo, g, wa, wb, wo, wg, wp, npl, ex)


def _rope_tables(s):
    inv = ROPE_THETA ** (-jnp.arange(0, ROT_DIM, 2, dtype=F32) / ROT_DIM)
    ang = jnp.arange(s, dtype=F32)[:, None] * inv[None, :]
    ang = jnp.concatenate([ang, ang], axis=-1)
    cos = jnp.cos(ang)
    sin = jnp.sin(ang)
    half = ROT_DIM // 2
    pad = HEAD_DIM - ROT_DIM
    c_head = jnp.concatenate([cos, jnp.ones((s, pad), F32)], axis=1)
    s1_head = jnp.concatenate([-sin[:, :half], jnp.zeros((s, HEAD_DIM - half), F32)], axis=1)
    s2_head = jnp.concatenate([jnp.zeros((s, half), F32), sin[:, half:], jnp.zeros((s, pad), F32)], axis=1)
    reps = LANES // HEAD_DIM
    return jnp.tile(c_head, (1, reps)), jnp.tile(s1_head, (1, reps)), jnp.tile(s2_head, (1, reps))


def _layer_constants():
    lane = jnp.arange(SLAB)
    ones = ((lane[:, None] // HEAD_DIM == lane[None, :] // HEAD_DIM) * (1.0 / HEAD_DIM)).astype(BF16)
    spread = (jnp.arange(LANES)[:, None] == jnp.arange(WIDTH_A)[None, :] // HEAD_DIM).astype(BF16)
    return ones, spread


def _trunk(x, p, norm_mix, w_in, a_q_norm, a_k_norm, b_q_norm, b_k_norm, b_sink, w_branch_a,
           w_branch_b, w_out, norm_ple, w_ple, w_ple_gate):
    b, s, _ = x.shape
    depth = w_in.shape[0]
    row_tile = min(ROW_TILE, s)
    c_t, s1_t, s2_t = _rope_tables(s)
    ones, spread = _layer_constants()
    scale = HEAD_DIM ** -0.5 * LOG2E
    heads = WIDTH_A // HEAD_DIM
    for i in range(depth):
        ga = jnp.concatenate([jnp.tile(a_q_norm[i] * scale, (1, heads)).reshape(1, -1),
                              jnp.tile(a_k_norm[i], (1, heads)).reshape(1, -1)], axis=1)
        gb = jnp.concatenate([jnp.tile(b_q_norm[i] * scale, N_HEADS_B),
                              jnp.tile(b_k_norm[i], KV_WIDTH_B // HEAD_DIM)])[None, :]
        qa0, qa1, qa2, qb, g = _project(x, norm_mix[i][None, :], w_in[i].astype(BF16), ga, gb,
                                        c_t, s1_t, s2_t, ones, row_tile)
        outs = [_attention_a(qkv, window, ATTN_TILE)
                for qkv, (window, _) in zip((qa0, qa1, qa2), DIL_PAIRS)]
        bo = _attention_b(qb, b_sink[i], ATTN_TILE)
        x = _merge(x, p, i, outs, bo, g,
                   w_branch_a[i].astype(BF16), w_branch_b[i].astype(BF16), w_out[i].astype(BF16),
                   w_ple_gate[i].astype(BF16), w_ple[i].astype(BF16), norm_ple[i][None, :],
                   spread, row_tile)
    return x


def kernel(x_prompt, x_sample, p_prompt, p_sample, norm_mix, w_in, a_q_norm, a_k_norm, b_q_norm,
           b_k_norm, b_sink, w_branch_a, w_branch_b, w_out, norm_ple, w_ple, w_ple_gate):
    weights = (norm_mix, w_in, a_q_norm, a_k_norm, b_q_norm, b_k_norm, b_sink, w_branch_a,
               w_branch_b, w_out, norm_ple, w_ple, w_ple_gate)
    return (_trunk(x_prompt, p_prompt, *weights), _trunk(x_sample, p_sample, *weights))
```

```python
import functools

import jax
import jax.numpy as jnp
from jax import lax
from jax.experimental import pallas as pl
from jax.experimental.pallas import tpu as pltpu

D_MODEL = 1024
HEAD_DIM = 64
ROT_DIM = HEAD_DIM // 4
ROPE_THETA = 500000.0
DIL_PAIRS = ((128, 1), (512, 4), (2048, 16))
N_GROUPS_A = 3
WIDTH_A = 512
WIDTH_B = 512
KV_WIDTH_B = 128
N_HEADS_B = 8
WINDOW_B = 128
PLE_DIM = 256
A_QKV_COLS = 3 * N_GROUPS_A * WIDTH_A
IN_COLS = A_QKV_COLS + WIDTH_A + WIDTH_B + 2 * KV_WIDTH_B + WIDTH_B + 2 * D_MODEL
EPS = 1e-6
NEG = -1e30
LOG2E = 1.4426950408889634

OFF_A_GATE = A_QKV_COLS
OFF_BQ = OFF_A_GATE + WIDTH_A
OFF_BK = OFF_BQ + WIDTH_B
OFF_BV = OFF_BK + KV_WIDTH_B
OFF_B_GATE = OFF_BV + KV_WIDTH_B
OFF_MG_A = OFF_B_GATE + WIDTH_B
OFF_MG_B = OFF_MG_A + D_MODEL

G_A_GATE = 0
G_B_GATE = G_A_GATE + WIDTH_A
G_MG_A = G_B_GATE + WIDTH_B
G_MG_B = G_MG_A + D_MODEL
G_COLS = G_MG_B + D_MODEL

LANES = 128
SLAB = 256
HEADS_PER_SLAB = SLAB // HEAD_DIM
Q_BLOCK = 128
ROW_TILE = 512
ATTN_TILE = 1024
VMEM_LIMIT = 56 * 1024 * 1024

BF16 = jnp.bfloat16
F32 = jnp.float32


def _residue_major(ref, dilation):
    rows = ref.shape[0] // dilation
    return jnp.concatenate([ref[pl.ds(r, rows, stride=dilation), :] for r in range(dilation)], axis=0)


def _proj_kernel(x_ref, nm_ref, w_ref, ga_ref, gb_ref, c_ref, s1_ref, s2_ref, ones_ref,
                 qa0_ref, qa1_ref, qa2_ref, qb_ref, g_ref, hslab):
    xf = x_ref[...]
    ms = jnp.mean(xf * xf, axis=-1, keepdims=True)
    hn32 = xf * lax.rsqrt(ms + EPS) * nm_ref[...]
    n_slabs = D_MODEL // LANES
    for j in range(n_slabs):
        hslab[j] = hn32[:, j * LANES:(j + 1) * LANES]
    ones = ones_ref[...]

    def hn_for(dilation):
        if dilation == 1:
            return hn32.astype(BF16)
        return jnp.concatenate([_residue_major(hslab.at[j], dilation) for j in range(n_slabs)],
                               axis=1).astype(BF16)

    def tables_for(dilation):
        if dilation == 1:
            return c_ref[...], s1_ref[...], s2_ref[...]
        return tuple(_residue_major(t, dilation) for t in (c_ref, s1_ref, s2_ref))

    def wide(t):
        return jnp.concatenate([t, t], axis=1)

    def norm_rope(t, gain, cc, sa, sb, ones_m):
        width = t.shape[1]
        ms = jnp.dot((t * t).astype(BF16), ones_m, preferred_element_type=F32)
        n = t * lax.rsqrt(ms + EPS) * gain
        up = pltpu.roll(n, width - ROT_DIM // 2, 1)
        dn = pltpu.roll(n, ROT_DIM // 2, 1)
        return n * cc + up * sa + dn * sb

    def proj(hn, lo, width):
        return jnp.dot(hn, w_ref[:, lo:lo + width], preferred_element_type=F32)

    def normed(hn, lo, gains, glo, tabs):
        t = proj(hn, lo, WIDTH_A)
        cc, sa, sb = (wide(t_) for t_ in tabs)
        halves = [norm_rope(t[:, h * SLAB:(h + 1) * SLAB], gains[:, glo + h * SLAB:glo + (h + 1) * SLAB],
                            cc, sa, sb, ones) for h in range(WIDTH_A // SLAB)]
        return jnp.concatenate(halves, axis=1).astype(BF16)

    for gi, (out_ref, (_, dilation)) in enumerate(zip((qa0_ref, qa1_ref, qa2_ref), DIL_PAIRS)):
        hn = hn_for(dilation)
        tabs = tables_for(dilation)
        q = normed(hn, gi * WIDTH_A, ga_ref, gi * WIDTH_A, tabs)
        k = normed(hn, (N_GROUPS_A + gi) * WIDTH_A, ga_ref, (N_GROUPS_A + gi) * WIDTH_A, tabs)
        v = proj(hn, (2 * N_GROUPS_A + gi) * WIDTH_A, WIDTH_A).astype(BF16)
        rows = out_ref.shape[1]
        for r in range(dilation):
            rs = slice(r * rows, (r + 1) * rows)
            out_ref[r, :, 0:WIDTH_A] = q[rs]
            out_ref[r, :, WIDTH_A:2 * WIDTH_A] = k[rs]
            out_ref[r, :, 2 * WIDTH_A:] = v[rs]

    hn = hn_for(1)
    tabs = tables_for(1)
    qb_ref[:, 0:WIDTH_B] = normed(hn, OFF_BQ, gb_ref, 0, tabs)
    t = proj(hn, OFF_BK, 2 * KV_WIDTH_B)
    kb = norm_rope(t[:, :KV_WIDTH_B], gb_ref[:, WIDTH_B:WIDTH_B + KV_WIDTH_B], tabs[0], tabs[1], tabs[2],
                   ones[:KV_WIDTH_B, :KV_WIDTH_B])
    qb_ref[:, WIDTH_B:WIDTH_B + KV_WIDTH_B] = kb.astype(BF16)
    qb_ref[:, WIDTH_B + KV_WIDTH_B:] = t[:, KV_WIDTH_B:].astype(BF16)
    g_ref[:, G_A_GATE:G_A_GATE + WIDTH_A] = proj(hn, OFF_A_GATE, WIDTH_A).astype(BF16)
    g_ref[:, G_B_GATE:G_B_GATE + WIDTH_B] = proj(hn, OFF_B_GATE, WIDTH_B).astype(BF16)
    for j in range(2 * D_MODEL // WIDTH_A):
        g_ref[:, G_MG_A + j * WIDTH_A:G_MG_A + (j + 1) * WIDTH_A] = proj(
            hn, OFF_MG_A + j * WIDTH_A, WIDTH_A).astype(BF16)


def _project(x3, nm, w, ga, gb, c_t, s1_t, s2_t, ones, tile):
    b, s, _ = x3.shape
    const = lambda bi, i: (0, 0)
    tab = lambda bi, i: (i, 0)
    row = lambda bi, i: (bi, i, 0)
    qa_specs, qa_shapes = [], []
    for _, d in DIL_PAIRS:
        qa_specs.append(pl.BlockSpec((None, d, tile // d, 3 * WIDTH_A), lambda bi, i: (bi, 0, i, 0)))
        qa_shapes.append(jax.ShapeDtypeStruct((b, d, s // d, 3 * WIDTH_A), BF16))
    return pl.pallas_call(
        _proj_kernel,
        grid=(b, s // tile),
        in_specs=[
            pl.BlockSpec((None, tile, D_MODEL), row),
            pl.BlockSpec((1, D_MODEL), const),
            pl.BlockSpec((D_MODEL, IN_COLS), const, pipeline_mode=pl.Buffered(1)),
            pl.BlockSpec((1, 2 * N_GROUPS_A * WIDTH_A), const),
            pl.BlockSpec((1, WIDTH_B + KV_WIDTH_B), const),
            pl.BlockSpec((tile, LANES), tab),
            pl.BlockSpec((tile, LANES), tab),
            pl.BlockSpec((tile, LANES), tab),
            pl.BlockSpec((SLAB, SLAB), const),
        ],
        out_specs=qa_specs + [
            pl.BlockSpec((None, tile, WIDTH_B + 2 * KV_WIDTH_B), row),
            pl.BlockSpec((None, tile, G_COLS), row),
        ],
        out_shape=qa_shapes + [
            jax.ShapeDtypeStruct((b, s, WIDTH_B + 2 * KV_WIDTH_B), BF16),
            jax.ShapeDtypeStruct((b, s, G_COLS), BF16),
        ],
        scratch_shapes=[pltpu.VMEM((D_MODEL // LANES, tile, LANES), F32)],
        compiler_params=pltpu.CompilerParams(
            dimension_semantics=("arbitrary", "arbitrary"), vmem_limit_bytes=VMEM_LIMIT),
        name="proj_in",
    )(x3, nm, w, ga, gb, c_t, s1_t, s2_t, ones)


def _band_bias(base, half_window, n_keys, length):
    qi = lax.broadcasted_iota(jnp.int32, (Q_BLOCK, 1), 0)
    jj = lax.broadcasted_iota(jnp.int32, (Q_BLOCK, n_keys), 1)
    lo, hi = qi, qi + 2 * half_window
    if base is not None:
        lo = jnp.maximum(lo, half_window - base)
        hi = jnp.minimum(hi, length - 1 - base + half_window)
    return jnp.where(jj >= lo, jnp.where(jj <= hi, 0.0, NEG), NEG).astype(F32)


def _block_biases(t0, n_blocks, half_window, n_keys, length):
    inner = _band_bias(None, half_window, n_keys, length) if n_blocks > 2 else None
    return [_band_bias(t0 + i * Q_BLOCK, half_window, n_keys, length) if i in (0, n_blocks - 1) else inner
            for i in range(n_blocks)]


def _window(prev_ref, main_ref, next_ref, start, n, cols):
    w = prev_ref.shape[0]
    tile = main_ref.shape[0]
    pieces = []
    if start < 0:
        pieces.append(prev_ref[w + start:w, cols])
    pieces.append(main_ref[max(start, 0):min(start + n, tile), cols])
    if start + n > tile:
        pieces.append(next_ref[0:start + n - tile, cols])
    return pieces[0] if len(pieces) == 1 else jnp.concatenate(pieces, axis=0)


def _slab_attention(q_slab, kw, vw, bias, sinks, normalize):
    head_of_lane = lax.broadcasted_iota(jnp.int32, (1, SLAB), 1) // HEAD_DIM
    heads = range(HEADS_PER_SLAB)
    q_bits = pltpu.bitcast(q_slab, jnp.uint32)
    keep = [jnp.where(head_of_lane == h, jnp.uint32(0xFFFFFFFF), jnp.uint32(0)) for h in heads]
    q4 = jnp.concatenate([pltpu.bitcast(q_bits & keep[h], BF16) for h in heads], axis=0)
    s4 = lax.dot_general(q4, kw, (((1,), (1,)), ((), ())), preferred_element_type=F32)
    probs, ms, ls = [], [], []
    for h in heads:
        s = s4[h * Q_BLOCK:(h + 1) * Q_BLOCK] + bias
        m = jnp.max(s, axis=-1, keepdims=True)
        e = jnp.exp2(s - m)
        l = jnp.sum(e, axis=-1, keepdims=True)
        if sinks is not None:
            l = l + jnp.exp2(sinks[h] - m)
        probs.append(e.astype(BF16))
        ms.append(m)
        ls.append(l)
    acc4 = jnp.dot(jnp.concatenate(probs, axis=0), vw, preferred_element_type=F32)
    out = None
    for h in reversed(heads):
        part = acc4[h * Q_BLOCK:(h + 1) * Q_BLOCK]
        if normalize:
            part = part * (1.0 / ls[h])
        out = part if out is None else jnp.where(head_of_lane == h, part, out)
    return out, ms, ls


def _attn_a_kernel(q_ref, km_ref, kp_ref, kn_ref, vm_ref, vp_ref, vn_ref, o_ref, m_ref, l_ref,
                   *, half_window, tile, length):
    w = half_window
    n_keys = Q_BLOCK + 2 * w
    n_blocks = tile // Q_BLOCK
    biases = _block_biases(pl.program_id(2) * tile, n_blocks, w, n_keys, length)
    lane = lax.broadcasted_iota(jnp.int32, (1, LANES), 1)
    for i in range(n_blocks):
        r0 = i * Q_BLOCK
        m_rows = jnp.zeros((Q_BLOCK, LANES), F32)
        l_rows = jnp.ones((Q_BLOCK, LANES), F32)
        for c in range(WIDTH_A // SLAB):
            cols = slice(c * SLAB, (c + 1) * SLAB)
            out, ms, ls = _slab_attention(q_ref[r0:r0 + Q_BLOCK, cols],
                                          _window(kp_ref, km_ref, kn_ref, r0 - w, n_keys, cols),
                                          _window(vp_ref, vm_ref, vn_ref, r0 - w, n_keys, cols),
                                          biases[i], None, normalize=False)
            o_ref[r0:r0 + Q_BLOCK, cols] = out.astype(BF16)
            for h in range(HEADS_PER_SLAB):
                m_rows = jnp.where(lane == c * HEADS_PER_SLAB + h, ms[h], m_rows)
                l_rows = jnp.where(lane == c * HEADS_PER_SLAB + h, ls[h], l_rows)
        m_ref[r0:r0 + Q_BLOCK, :] = m_rows
        l_ref[r0:r0 + Q_BLOCK, :] = l_rows


def _attention_a(qkv, window, tile):
    b, dilation, length, _ = qkv.shape
    half_window = (window // 2) // dilation
    tile = min(tile, length)
    halo_per_tile = tile // half_window
    n_halo = length // half_window

    def main(sec):
        return pl.BlockSpec((None, None, tile, WIDTH_A), lambda bi, r, l: (bi, r, l, sec))

    def prev(sec):
        return pl.BlockSpec((None, None, half_window, WIDTH_A),
                            lambda bi, r, l: (bi, r, jnp.maximum(l * halo_per_tile - 1, 0), sec))

    def nxt(sec):
        return pl.BlockSpec((None, None, half_window, WIDTH_A),
                            lambda bi, r, l: (bi, r, jnp.minimum((l + 1) * halo_per_tile, n_halo - 1), sec))

    return pl.pallas_call(
        functools.partial(_attn_a_kernel, half_window=half_window, tile=tile, length=length),
        grid=(b, dilation, length // tile),
        in_specs=[main(0), main(1), prev(1), nxt(1), main(2), prev(2), nxt(2)],
        out_specs=[
            pl.BlockSpec((None, None, tile, WIDTH_A), lambda bi, r, l: (bi, r, l, 0)),
            pl.BlockSpec((None, None, tile, LANES), lambda bi, r, l: (bi, r, l, 0)),
            pl.BlockSpec((None, None, tile, LANES), lambda bi, r, l: (bi, r, l, 0)),
        ],
        out_shape=[
            jax.ShapeDtypeStruct((b, dilation, length, WIDTH_A), BF16),
            jax.ShapeDtypeStruct((b, dilation, length, LANES), F32),
            jax.ShapeDtypeStruct((b, dilation, length, LANES), F32),
        ],
        compiler_params=pltpu.CompilerParams(
            dimension_semantics=("arbitrary", "arbitrary", "arbitrary"),
            vmem_limit_bytes=VMEM_LIMIT),
        name=f"attn_a_d{dilation}",
    )(qkv, qkv, qkv, qkv, qkv, qkv, qkv)


def _attn_b_kernel(sink_ref, q_ref, km_ref, kp_ref, kn_ref, vm_ref, vp_ref, vn_ref, o_ref,
                   *, tile, length):
    w = WINDOW_B
    n_keys = Q_BLOCK + 2 * w
    n_blocks = tile // Q_BLOCK
    biases = _block_biases(pl.program_id(1) * tile, n_blocks, w, n_keys, length)
    even_block = (lax.broadcasted_iota(jnp.int32, (1, SLAB), 1) // HEAD_DIM) % 2 == 0
    all_lanes = slice(0, KV_WIDTH_B)

    def per_kv_head(x):
        swapped = jnp.concatenate([x[:, HEAD_DIM:], x[:, :HEAD_DIM]], axis=1)
        pair = jnp.concatenate([x, x], axis=1)
        riap = jnp.concatenate([swapped, swapped], axis=1)
        return jnp.where(even_block, pair, riap), jnp.where(even_block, riap, pair)

    for i in range(n_blocks):
        r0 = i * Q_BLOCK
        k_heads = per_kv_head(_window(kp_ref, km_ref, kn_ref, r0 - w, n_keys, all_lanes))
        v_heads = per_kv_head(_window(vp_ref, vm_ref, vn_ref, r0 - w, n_keys, all_lanes))
        for c in range(WIDTH_B // SLAB):
            sinks = [sink_ref[c * HEADS_PER_SLAB + h] * LOG2E for h in range(HEADS_PER_SLAB)]
            cols = slice(c * SLAB, (c + 1) * SLAB)
            out, _, _ = _slab_attention(q_ref[r0:r0 + Q_BLOCK, cols], k_heads[c], v_heads[c],
                                        biases[i], sinks, normalize=True)
            o_ref[r0:r0 + Q_BLOCK, cols] = out.astype(BF16)


def _attention_b(qb3, sink, tile):
    b, s, _ = qb3.shape
    tile = min(tile, s)
    halo_per_tile = tile // WINDOW_B
    n_halo = s // WINDOW_B
    k_col = WIDTH_B // KV_WIDTH_B
    v_col = k_col + 1

    def main(col):
        return pl.BlockSpec((None, tile, KV_WIDTH_B), lambda bi, l, sk: (bi, l, col))

    def prev(col):
        return pl.BlockSpec((None, WINDOW_B, KV_WIDTH_B),
                            lambda bi, l, sk: (bi, jnp.maximum(l * halo_per_tile - 1, 0), col))

    def nxt(col):
        return pl.BlockSpec((None, WINDOW_B, KV_WIDTH_B),
                            lambda bi, l, sk: (bi, jnp.minimum((l + 1) * halo_per_tile, n_halo - 1), col))

    return pl.pallas_call(
        functools.partial(_attn_b_kernel, tile=tile, length=s),
        grid_spec=pltpu.PrefetchScalarGridSpec(
            num_scalar_prefetch=1,
            grid=(b, s // tile),
            in_specs=[
                pl.BlockSpec((None, tile, WIDTH_B), lambda bi, l, sk: (bi, l, 0)),
                main(k_col), prev(k_col), nxt(k_col), main(v_col), prev(v_col), nxt(v_col),
            ],
            out_specs=pl.BlockSpec((None, tile, WIDTH_B), lambda bi, l, sk: (bi, l, 0)),
        ),
        out_shape=jax.ShapeDtypeStruct((b, s, WIDTH_B), BF16),
        compiler_params=pltpu.CompilerParams(
            dimension_semantics=("arbitrary", "arbitrary"), vmem_limit_bytes=VMEM_LIMIT),
        name="attn_b",
    )(sink, qb3, qb3, qb3, qb3, qb3, qb3, qb3)


def _sigmoid(x):
    half = jnp.asarray(0.5, x.dtype)
    return half * jnp.tanh(half * x) + half


def _merge_kernel(x_ref, p_ref, o0_ref, o1_ref, o2_ref, m0_ref, m1_ref, m2_ref, l0_ref, l1_ref, l2_ref,
                  bo_ref, g_ref, wa_ref, wb_ref, wo_ref, wg_ref, wp_ref, np_ref, ex_ref, y_ref,
                  oslab, mslab, lslab):
    n_slabs = WIDTH_A // LANES

    def token_major(o_ref, m_ref, l_ref, dilation, slot):
        if dilation == 1:
            return o_ref[0].astype(F32), m_ref[0], l_ref[0]
        rows = o_ref.shape[1]
        for r in range(dilation):
            o_r = o_ref[r].astype(F32)
            for j in range(n_slabs):
                oslab[slot, j, pl.ds(r, rows, stride=dilation), :] = o_r[:, j * LANES:(j + 1) * LANES]
            mslab[slot, pl.ds(r, rows, stride=dilation), :] = m_ref[r]
            lslab[slot, pl.ds(r, rows, stride=dilation), :] = l_ref[r]
        return (jnp.concatenate([oslab[slot, j] for j in range(n_slabs)], axis=1), mslab[slot], lslab[slot])

    groups = [token_major(o_ref, m_ref, l_ref, d, slot) for slot, (o_ref, m_ref, l_ref, (_, d)) in
              enumerate(zip((o0_ref, o1_ref, o2_ref), (m0_ref, m1_ref, m2_ref),
                            (l0_ref, l1_ref, l2_ref), DIL_PAIRS))]
    ex = ex_ref[...]

    def spread(wt):
        return jnp.dot(wt.astype(BF16), ex, preferred_element_type=F32)

    mx = jnp.maximum(jnp.maximum(groups[0][1], groups[1][1]), groups[2][1])
    es = [jnp.exp2(m - mx) for _, m, _ in groups]
    inv = 1.0 / (es[0] * groups[0][2] + es[1] * groups[1][2] + es[2] * groups[2][2])
    a_out = (spread(es[0] * inv) * groups[0][0] + spread(es[1] * inv) * groups[1][0]
             + spread(es[2] * inv) * groups[2][0])
    a_gate = g_ref[:, G_A_GATE:G_A_GATE + WIDTH_A]
    b_gate = g_ref[:, G_B_GATE:G_B_GATE + WIDTH_B]
    a_in = (a_out * (a_gate * _sigmoid(a_gate)).astype(F32)).astype(BF16)
    b_in = bo_ref[...] * (b_gate * _sigmoid(b_gate))
    a_y = jnp.dot(a_in, wa_ref[...], preferred_element_type=F32)
    b_y = jnp.dot(b_in, wb_ref[...], preferred_element_type=F32)
    merged = (_sigmoid(g_ref[:, G_MG_A:G_MG_A + D_MODEL]).astype(F32) * a_y
              + _sigmoid(g_ref[:, G_MG_B:G_MG_B + D_MODEL]).astype(F32) * b_y)
    x1 = x_ref[...] + jnp.dot(merged.astype(BF16), wo_ref[...], preferred_element_type=F32)
    ms = jnp.mean(x1 * x1, axis=-1, keepdims=True)
    xn = (x1 * lax.rsqrt(ms + EPS) * np_ref[...]).astype(BF16)
    gate = _sigmoid(jnp.dot(xn, wg_ref[...], preferred_element_type=F32))
    ple = jnp.dot(p_ref[...].astype(BF16), wp_ref[...], preferred_element_type=F32)
    y_ref[...] = x1 + gate * ple


def _merge(x3, p4, layer, outs, bo, g, wa, wb, wo, wg, wp, npl, ex, tile):
    b, s, _ = x3.shape
    row = lambda bi, i: (bi, i, 0)
    const = lambda bi, i: (0, 0)

    def rows(width):
        return pl.BlockSpec((None, tile, width), row)

    def grouped(width, d):
        return pl.BlockSpec((None, d, tile // d, width), lambda bi, i: (bi, 0, i, 0))

    def whole(r, c):
        return pl.BlockSpec((r, c), const, pipeline_mode=pl.Buffered(1))

    dils = [d for _, d in DIL_PAIRS]
    return pl.pallas_call(
        _merge_kernel,
        grid=(b, s // tile),
        in_specs=[rows(D_MODEL), pl.BlockSpec((None, None, tile, PLE_DIM), lambda bi, i: (layer, bi, i, 0))]
        + [grouped(WIDTH_A, d) for d in dils] + 2 * [grouped(LANES, d) for d in dils]
        + [rows(WIDTH_B), rows(G_COLS),
           whole(WIDTH_A, D_MODEL), whole(WIDTH_B, D_MODEL), whole(D_MODEL, D_MODEL),
           whole(D_MODEL, D_MODEL), whole(PLE_DIM, D_MODEL), whole(1, D_MODEL),
           whole(LANES, WIDTH_A)],
        out_specs=rows(D_MODEL),
        out_shape=jax.ShapeDtypeStruct((b, s, D_MODEL), F32),
        scratch_shapes=[
            pltpu.VMEM((N_GROUPS_A, WIDTH_A // LANES, tile, LANES), F32),
            pltpu.VMEM((N_GROUPS_A, tile, LANES), F32),
            pltpu.VMEM((N_GROUPS_A, tile, LANES), F32),
        ],
        compiler_params=pltpu.CompilerParams(
            dimension_semantics=("arbitrary", "arbitrary"), vmem_limit_bytes=VMEM_LIMIT),
        name="merge_out",
    )(x3, p4, *[o for o, _, _ in outs], *[m for _, m, _ in outs], *[l for _, _, l in outs],
      bo, g, wa, wb, wo, wg, wp, npl, ex)


def _rope_tables(s):
    inv = ROPE_THETA ** (-jnp.arange(0, ROT_DIM, 2, dtype=F32) / ROT_DIM)
    ang = jnp.arange(s, dtype=F32)[:, None] * inv[None, :]
    ang = jnp.concatenate([ang, ang], axis=-1)
    cos = jnp.cos(ang)
    sin = jnp.sin(ang)
    half = ROT_DIM // 2
    pad = HEAD_DIM - ROT_DIM
    c_head = jnp.concatenate([cos, jnp.ones((s, pad), F32)], axis=1)
    s1_head = jnp.concatenate([-sin[:, :half], jnp.zeros((s, HEAD_DIM - half), F32)], axis=1)
    s2_head = jnp.concatenate([jnp.zeros((s, half), F32), sin[:, half:], jnp.zeros((s, pad), F32)], axis=1)
    reps = LANES // HEAD_DIM
    return jnp.tile(c_head, (1, reps)), jnp.tile(s1_head, (1, reps)), jnp.tile(s2_head, (1, reps))


def _layer_constants():
    lane = jnp.arange(SLAB)
    ones = ((lane[:, None] // HEAD_DIM == lane[None, :] // HEAD_DIM) * (1.0 / HEAD_DIM)).astype(BF16)
    spread = (jnp.arange(LANES)[:, None] == jnp.arange(WIDTH_A)[None, :] // HEAD_DIM).astype(BF16)
    return ones, spread


def _trunk(x, p, norm_mix, w_in, a_q_norm, a_k_norm, b_q_norm, b_k_norm, b_sink, w_branch_a,
           w_branch_b, w_out, norm_ple, w_ple, w_ple_gate):
    b, s, _ = x.shape
    depth = w_in.shape[0]
    row_tile = min(ROW_TILE, s)
    c_t, s1_t, s2_t = _rope_tables(s)
    ones, spread = _layer_constants()
    scale = HEAD_DIM ** -0.5 * LOG2E
    heads = WIDTH_A // HEAD_DIM
    for i in range(depth):
        ga = jnp.concatenate([jnp.tile(a_q_norm[i] * scale, (1, heads)).reshape(1, -1),
                              jnp.tile(a_k_norm[i], (1, heads)).reshape(1, -1)], axis=1)
        gb = jnp.concatenate([jnp.tile(b_q_norm[i] * scale, N_HEADS_B),
                              jnp.tile(b_k_norm[i], KV_WIDTH_B // HEAD_DIM)])[None, :]
        qa0, qa1, qa2, qb, g = _project(x, norm_mix[i][None, :], w_in[i].astype(BF16), ga, gb,
                                        c_t, s1_t, s2_t, ones, row_tile)
        outs = [_attention_a(qkv, window, ATTN_TILE)
                for qkv, (window, _) in zip((qa0, qa1, qa2), DIL_PAIRS)]
        bo = _attention_b(qb, b_sink[i], ATTN_TILE)
        x = _merge(x, p, i, outs, bo, g,
                   w_branch_a[i].astype(BF16), w_branch_b[i].astype(BF16), w_out[i].astype(BF16),
                   w_ple_gate[i].astype(BF16), w_ple[i].astype(BF16), norm_ple[i][None, :],
                   spread, row_tile)
    return x


def kernel(x_prompt, x_sample, p_prompt, p_sample, norm_mix, w_in, a_q_norm, a_k_norm, b_q_norm,
           b_k_norm, b_sink, w_branch_a, w_branch_b, w_out, norm_ple, w_ple, w_ple_gate):
    weights = (norm_mix, w_in, a_q_norm, a_k_norm, b_q_norm, b_k_norm, b_sink, w_branch_a,
               w_branch_b, w_out, norm_ple, w_ple, w_ple_gate)
    return (_trunk(x_prompt, p_prompt, *weights), _trunk(x_sample, p_sample, *weights))
```

```python
import functools

import jax
import jax.numpy as jnp
from jax import lax
from jax.experimental import pallas as pl
from jax.experimental.pallas import tpu as pltpu

D_MODEL = 1024
HEAD_DIM = 64
ROT_DIM = HEAD_DIM // 4
ROPE_THETA = 500000.0
DIL_PAIRS = ((128, 1), (512, 4), (2048, 16))
N_GROUPS_A = 3
WIDTH_A = 512
WIDTH_B = 512
KV_WIDTH_B = 128
N_HEADS_B = 8
WINDOW_B = 128
PLE_DIM = 256
A_QKV_COLS = 3 * N_GROUPS_A * WIDTH_A
IN_COLS = A_QKV_COLS + WIDTH_A + WIDTH_B + 2 * KV_WIDTH_B + WIDTH_B + 2 * D_MODEL
EPS = 1e-6
NEG = -1e30
LOG2E = 1.4426950408889634

OFF_A_GATE = A_QKV_COLS
OFF_BQ = OFF_A_GATE + WIDTH_A
OFF_BK = OFF_BQ + WIDTH_B
OFF_BV = OFF_BK + KV_WIDTH_B
OFF_B_GATE = OFF_BV + KV_WIDTH_B
OFF_MG_A = OFF_B_GATE + WIDTH_B
OFF_MG_B = OFF_MG_A + D_MODEL

G_A_GATE = 0
G_B_GATE = G_A_GATE + WIDTH_A
G_MG_A = G_B_GATE + WIDTH_B
G_MG_B = G_MG_A + D_MODEL
G_COLS = G_MG_B + D_MODEL

LANES = 128
SLAB = 256
HEADS_PER_SLAB = SLAB // HEAD_DIM
Q_BLOCK = 128
ROW_TILE = 512
ATTN_TILE = 1024
VMEM_LIMIT = 56 * 1024 * 1024

BF16 = jnp.bfloat16
F32 = jnp.float32


def _residue_major(ref, dilation):
    rows = ref.shape[0] // dilation
    return jnp.concatenate([ref[pl.ds(r, rows, stride=dilation), :] for r in range(dilation)], axis=0)


def _proj_kernel(x_ref, nm_ref, w_ref, ga_ref, gb_ref, c_ref, s1_ref, s2_ref, ones_ref,
                 qa0_ref, qa1_ref, qa2_ref, qb_ref, g_ref, hslab):
    xf = x_ref[...]
    ms = jnp.mean(xf * xf, axis=-1, keepdims=True)
    hn32 = xf * lax.rsqrt(ms + EPS) * nm_ref[...]
    n_slabs = D_MODEL // LANES
    for j in range(n_slabs):
        hslab[j] = hn32[:, j * LANES:(j + 1) * LANES]
    ones = ones_ref[...]

    def hn_for(dilation):
        if dilation == 1:
            return hn32.astype(BF16)
        return jnp.concatenate([_residue_major(hslab.at[j], dilation) for j in range(n_slabs)],
                               axis=1).astype(BF16)

    def tables_for(dilation):
        if dilation == 1:
            return c_ref[...], s1_ref[...], s2_ref[...]
        return tuple(_residue_major(t, dilation) for t in (c_ref, s1_ref, s2_ref))

    def wide(t):
        return jnp.concatenate([t, t], axis=1)

    def norm_rope(t, gain, cc, sa, sb, ones_m):
        width = t.shape[1]
        ms = jnp.dot((t * t).astype(BF16), ones_m, preferred_element_type=F32)
        n = t * lax.rsqrt(ms + EPS) * gain
        up = pltpu.roll(n, width - ROT_DIM // 2, 1)
        dn = pltpu.roll(n, ROT_DIM // 2, 1)
        return n * cc + up * sa + dn * sb

    def proj(hn, lo, width):
        return jnp.dot(hn, w_ref[:, lo:lo + width], preferred_element_type=F32)

    def normed(hn, lo, gains, glo, tabs):
        t = proj(hn, lo, WIDTH_A)
        cc, sa, sb = (wide(t_) for t_ in tabs)
        halves = [norm_rope(t[:, h * SLAB:(h + 1) * SLAB], gains[:, glo + h * SLAB:glo + (h + 1) * SLAB],
                            cc, sa, sb, ones) for h in range(WIDTH_A // SLAB)]
        return jnp.concatenate(halves, axis=1).astype(BF16)

    for gi, (out_ref, (_, dilation)) in enumerate(zip((qa0_ref, qa1_ref, qa2_ref), DIL_PAIRS)):
        hn = hn_for(dilation)
        tabs = tables_for(dilation)
        q = normed(hn, gi * WIDTH_A, ga_ref, gi * WIDTH_A, tabs)
        k = normed(hn, (N_GROUPS_A + gi) * WIDTH_A, ga_ref, (N_GROUPS_A + gi) * WIDTH_A, tabs)
        v = proj(hn, (2 * N_GROUPS_A + gi) * WIDTH_A, WIDTH_A).astype(BF16)
        rows = out_ref.shape[1]
        for r in range(dilation):
            rs = slice(r * rows, (r + 1) * rows)
            out_ref[r, :, 0:WIDTH_A] = q[rs]
            out_ref[r, :, WIDTH_A:2 * WIDTH_A] = k[rs]
            out_ref[r, :, 2 * WIDTH_A:] = v[rs]

    hn = hn_for(1)
    tabs = tables_for(1)
    qb_ref[:, 0:WIDTH_B] = normed(hn, OFF_BQ, gb_ref, 0, tabs)
    t = proj(hn, OFF_BK, 2 * KV_WIDTH_B)
    kb = norm_rope(t[:, :KV_WIDTH_B], gb_ref[:, WIDTH_B:WIDTH_B + KV_WIDTH_B], tabs[0], tabs[1], tabs[2],
                   ones[:KV_WIDTH_B, :KV_WIDTH_B])
    qb_ref[:, WIDTH_B:WIDTH_B + KV_WIDTH_B] = kb.astype(BF16)
    qb_ref[:, WIDTH_B + KV_WIDTH_B:] = t[:, KV_WIDTH_B:].astype(BF16)
    g_ref[:, G_A_GATE:G_A_GATE + WIDTH_A] = proj(hn, OFF_A_GATE, WIDTH_A).astype(BF16)
    g_ref[:, G_B_GATE:G_B_GATE + WIDTH_B] = proj(hn, OFF_B_GATE, WIDTH_B).astype(BF16)
    for j in range(2 * D_MODEL // WIDTH_A):
        g_ref[:, G_MG_A + j * WIDTH_A:G_MG_A + (j + 1) * WIDTH_A] = proj(
            hn, OFF_MG_A + j * WIDTH_A, WIDTH_A).astype(BF16)


def _project(x3, nm, w, ga, gb, c_t, s1_t, s2_t, ones, tile):
    b, s, _ = x3.shape
    const = lambda bi, i: (0, 0)
    tab = lambda bi, i: (i, 0)
    row = lambda bi, i: (bi, i, 0)
    qa_specs, qa_shapes = [], []
    for _, d in DIL_PAIRS:
        qa_specs.append(pl.BlockSpec((None, d, tile // d, 3 * WIDTH_A), lambda bi, i: (bi, 0, i, 0)))
        qa_shapes.append(jax.ShapeDtypeStruct((b, d, s // d, 3 * WIDTH_A), BF16))
    return pl.pallas_call(
        _proj_kernel,
        grid=(b, s // tile),
        in_specs=[
            pl.BlockSpec((None, tile, D_MODEL), row),
            pl.BlockSpec((1, D_MODEL), const),
            pl.BlockSpec((D_MODEL, IN_COLS), const, pipeline_mode=pl.Buffered(1)),
            pl.BlockSpec((1, 2 * N_GROUPS_A * WIDTH_A), const),
            pl.BlockSpec((1, WIDTH_B + KV_WIDTH_B), const),
            pl.BlockSpec((tile, LANES), tab),
            pl.BlockSpec((tile, LANES), tab),
            pl.BlockSpec((tile, LANES), tab),
            pl.BlockSpec((SLAB, SLAB), const),
        ],
        out_specs=qa_specs + [
            pl.BlockSpec((None, tile, WIDTH_B + 2 * KV_WIDTH_B), row),
            pl.BlockSpec((None, tile, G_COLS), row),
        ],
        out_shape=qa_shapes + [
            jax.ShapeDtypeStruct((b, s, WIDTH_B + 2 * KV_WIDTH_B), BF16),
            jax.ShapeDtypeStruct((b, s, G_COLS), BF16),
        ],
        scratch_shapes=[pltpu.VMEM((D_MODEL // LANES, tile, LANES), F32)],
        compiler_params=pltpu.CompilerParams(
            dimension_semantics=("arbitrary", "arbitrary"), vmem_limit_bytes=VMEM_LIMIT),
        name="proj_in",
    )(x3, nm, w, ga, gb, c_t, s1_t, s2_t, ones)


def _band_bias(base, half_window, n_keys, length):
    qi = lax.broadcasted_iota(jnp.int32, (Q_BLOCK, 1), 0)
    jj = lax.broadcasted_iota(jnp.int32, (Q_BLOCK, n_keys), 1)
    lo, hi = qi, qi + 2 * half_window
    if base is not None:
        lo = jnp.maximum(lo, half_window - base)
        hi = jnp.minimum(hi, length - 1 - base + half_window)
    return jnp.where(jj >= lo, jnp.where(jj <= hi, 0.0, NEG), NEG).astype(F32)


def _block_biases(t0, n_blocks, half_window, n_keys, length):
    inner = _band_bias(None, half_window, n_keys, length) if n_blocks > 2 else None
    return [_band_bias(t0 + i * Q_BLOCK, half_window, n_keys, length) if i in (0, n_blocks - 1) else inner
            for i in range(n_blocks)]


def _window(prev_ref, main_ref, next_ref, start, n, cols):
    w = prev_ref.shape[0]
    tile = main_ref.shape[0]
    pieces = []
    if start < 0:
        pieces.append(prev_ref[w + start:w, cols])
    pieces.append(main_ref[max(start, 0):min(start + n, tile), cols])
    if start + n > tile:
        pieces.append(next_ref[0:start + n - tile, cols])
    return pieces[0] if len(pieces) == 1 else jnp.concatenate(pieces, axis=0)


def _slab_attention(q_slab, kw, vw, bias, sinks, normalize):
    head_of_lane = lax.broadcasted_iota(jnp.int32, (1, SLAB), 1) // HEAD_DIM
    heads = range(HEADS_PER_SLAB)
    q_bits = pltpu.bitcast(q_slab, jnp.uint32)
    keep = [jnp.where(head_of_lane == h, jnp.uint32(0xFFFFFFFF), jnp.uint32(0)) for h in heads]
    q4 = jnp.concatenate([pltpu.bitcast(q_bits & keep[h], BF16) for h in heads], axis=0)
    s4 = lax.dot_general(q4, kw, (((1,), (1,)), ((), ())), preferred_element_type=F32)
    probs, ms, ls = [], [], []
    for h in heads:
        s = s4[h * Q_BLOCK:(h + 1) * Q_BLOCK] + bias
        m = jnp.max(s, axis=-1, keepdims=True)
        e = jnp.exp2(s - m)
        l = jnp.sum(e, axis=-1, keepdims=True)
        if sinks is not None:
            l = l + jnp.exp2(sinks[h] - m)
        probs.append(e.astype(BF16))
        ms.append(m)
        ls.append(l)
    acc4 = jnp.dot(jnp.concatenate(probs, axis=0), vw, preferred_element_type=F32)
    first_head = lax.broadcasted_iota(jnp.int32, (1, LANES), 1) < HEAD_DIM
    columns = []
    for c in range(SLAB // LANES):
        pair = []
        for h in (2 * c, 2 * c + 1):
            part = acc4[h * Q_BLOCK:(h + 1) * Q_BLOCK, c * LANES:(c + 1) * LANES]
            pair.append(part * (1.0 / ls[h]) if normalize else part)
        columns.append(jnp.where(first_head, pair[0], pair[1]))
    return jnp.concatenate(columns, axis=1), ms, ls


def _attn_a_kernel(q_ref, km_ref, kp_ref, kn_ref, vm_ref, vp_ref, vn_ref, o_ref, m_ref, l_ref,
                   *, half_window, tile, length):
    w = half_window
    n_keys = Q_BLOCK + 2 * w
    n_blocks = tile // Q_BLOCK
    biases = _block_biases(pl.program_id(2) * tile, n_blocks, w, n_keys, length)
    lane = lax.broadcasted_iota(jnp.int32, (1, LANES), 1)
    for i in range(n_blocks):
        r0 = i * Q_BLOCK
        m_rows = jnp.zeros((Q_BLOCK, LANES), F32)
        l_rows = jnp.ones((Q_BLOCK, LANES), F32)
        for c in range(WIDTH_A // SLAB):
            cols = slice(c * SLAB, (c + 1) * SLAB)
            out, ms, ls = _slab_attention(q_ref[r0:r0 + Q_BLOCK, cols],
                                          _window(kp_ref, km_ref, kn_ref, r0 - w, n_keys, cols),
                                          _window(vp_ref, vm_ref, vn_ref, r0 - w, n_keys, cols),
                                          biases[i], None, normalize=False)
            o_ref[r0:r0 + Q_BLOCK, cols] = out.astype(BF16)
            for h in range(HEADS_PER_SLAB):
                m_rows = jnp.where(lane == c * HEADS_PER_SLAB + h, ms[h], m_rows)
                l_rows = jnp.where(lane == c * HEADS_PER_SLAB + h, ls[h], l_rows)
        m_ref[r0:r0 + Q_BLOCK, :] = m_rows
        l_ref[r0:r0 + Q_BLOCK, :] = l_rows


def _attention_a(qkv, window, tile):
    b, dilation, length, _ = qkv.shape
    half_window = (window // 2) // dilation
    tile = min(tile, length)
    halo_per_tile = tile // half_window
    n_halo = length // half_window

    def main(sec):
        return pl.BlockSpec((None, None, tile, WIDTH_A), lambda bi, r, l: (bi, r, l, sec))

    def prev(sec):
        return pl.BlockSpec((None, None, half_window, WIDTH_A),
                            lambda bi, r, l: (bi, r, jnp.maximum(l * halo_per_tile - 1, 0), sec))

    def nxt(sec):
        return pl.BlockSpec((None, None, half_window, WIDTH_A),
                            lambda bi, r, l: (bi, r, jnp.minimum((l + 1) * halo_per_tile, n_halo - 1), sec))

    return pl.pallas_call(
        functools.partial(_attn_a_kernel, half_window=half_window, tile=tile, length=length),
        grid=(b, dilation, length // tile),
        in_specs=[main(0), main(1), prev(1), nxt(1), main(2), prev(2), nxt(2)],
        out_specs=[
            pl.BlockSpec((None, None, tile, WIDTH_A), lambda bi, r, l: (bi, r, l, 0)),
            pl.BlockSpec((None, None, tile, LANES), lambda bi, r, l: (bi, r, l, 0)),
            pl.BlockSpec((None, None, tile, LANES), lambda bi, r, l: (bi, r, l, 0)),
        ],
        out_shape=[
            jax.ShapeDtypeStruct((b, dilation, length, WIDTH_A), BF16),
            jax.ShapeDtypeStruct((b, dilation, length, LANES), F32),
            jax.ShapeDtypeStruct((b, dilation, length, LANES), F32),
        ],
        compiler_params=pltpu.CompilerParams(
            dimension_semantics=("arbitrary", "arbitrary", "arbitrary"),
            vmem_limit_bytes=VMEM_LIMIT),
        name=f"attn_a_d{dilation}",
    )(qkv, qkv, qkv, qkv, qkv, qkv, qkv)


def _attn_b_kernel(sink_ref, q_ref, km_ref, kp_ref, kn_ref, vm_ref, vp_ref, vn_ref, o_ref,
                   *, tile, length):
    w = WINDOW_B
    n_keys = Q_BLOCK + 2 * w
    n_blocks = tile // Q_BLOCK
    biases = _block_biases(pl.program_id(1) * tile, n_blocks, w, n_keys, length)
    even_block = (lax.broadcasted_iota(jnp.int32, (1, SLAB), 1) // HEAD_DIM) % 2 == 0
    all_lanes = slice(0, KV_WIDTH_B)

    def per_kv_head(x):
        swapped = jnp.concatenate([x[:, HEAD_DIM:], x[:, :HEAD_DIM]], axis=1)
        pair = jnp.concatenate([x, x], axis=1)
        riap = jnp.concatenate([swapped, swapped], axis=1)
        return jnp.where(even_block, pair, riap), jnp.where(even_block, riap, pair)

    for i in range(n_blocks):
        r0 = i * Q_BLOCK
        k_heads = per_kv_head(_window(kp_ref, km_ref, kn_ref, r0 - w, n_keys, all_lanes))
        v_heads = per_kv_head(_window(vp_ref, vm_ref, vn_ref, r0 - w, n_keys, all_lanes))
        for c in range(WIDTH_B // SLAB):
            sinks = [sink_ref[c * HEADS_PER_SLAB + h] * LOG2E for h in range(HEADS_PER_SLAB)]
            cols = slice(c * SLAB, (c + 1) * SLAB)
            out, _, _ = _slab_attention(q_ref[r0:r0 + Q_BLOCK, cols], k_heads[c], v_heads[c],
                                        biases[i], sinks, normalize=True)
            o_ref[r0:r0 + Q_BLOCK, cols] = out.astype(BF16)


def _attention_b(qb3, sink, tile):
    b, s, _ = qb3.shape
    tile = min(tile, s)
    halo_per_tile = tile // WINDOW_B
    n_halo = s // WINDOW_B
    k_col = WIDTH_B // KV_WIDTH_B
    v_col = k_col + 1

    def main(col):
        return pl.BlockSpec((None, tile, KV_WIDTH_B), lambda bi, l, sk: (bi, l, col))

    def prev(col):
        return pl.BlockSpec((None, WINDOW_B, KV_WIDTH_B),
                            lambda bi, l, sk: (bi, jnp.maximum(l * halo_per_tile - 1, 0), col))

    def nxt(col):
        return pl.BlockSpec((None, WINDOW_B, KV_WIDTH_B),
                            lambda bi, l, sk: (bi, jnp.minimum((l + 1) * halo_per_tile, n_halo - 1), col))

    return pl.pallas_call(
        functools.partial(_attn_b_kernel, tile=tile, length=s),
        grid_spec=pltpu.PrefetchScalarGridSpec(
            num_scalar_prefetch=1,
            grid=(b, s // tile),
            in_specs=[
                pl.BlockSpec((None, tile, WIDTH_B), lambda bi, l, sk: (bi, l, 0)),
                main(k_col), prev(k_col), nxt(k_col), main(v_col), prev(v_col), nxt(v_col),
            ],
            out_specs=pl.BlockSpec((None, tile, WIDTH_B), lambda bi, l, sk: (bi, l, 0)),
        ),
        out_shape=jax.ShapeDtypeStruct((b, s, WIDTH_B), BF16),
        compiler_params=pltpu.CompilerParams(
            dimension_semantics=("arbitrary", "arbitrary"), vmem_limit_bytes=VMEM_LIMIT),
        name="attn_b",
    )(sink, qb3, qb3, qb3, qb3, qb3, qb3, qb3)


def _sigmoid(x):
    half = jnp.asarray(0.5, x.dtype)
    return half * jnp.tanh(half * x) + half


def _merge_kernel(x_ref, p_ref, o0_ref, o1_ref, o2_ref, m0_ref, m1_ref, m2_ref, l0_ref, l1_ref, l2_ref,
                  bo_ref, g_ref, wa_ref, wb_ref, wo_ref, wg_ref, wp_ref, np_ref, ex_ref, y_ref,
                  oslab, mslab, lslab):
    n_slabs = WIDTH_A // LANES

    def token_major(o_ref, m_ref, l_ref, dilation, slot):
        if dilation == 1:
            return o_ref[0].astype(F32), m_ref[0], l_ref[0]
        rows = o_ref.shape[1]
        for r in range(dilation):
            o_r = o_ref[r].astype(F32)
            for j in range(n_slabs):
                oslab[slot, j, pl.ds(r, rows, stride=dilation), :] = o_r[:, j * LANES:(j + 1) * LANES]
            mslab[slot, pl.ds(r, rows, stride=dilation), :] = m_ref[r]
            lslab[slot, pl.ds(r, rows, stride=dilation), :] = l_ref[r]
        return (jnp.concatenate([oslab[slot, j] for j in range(n_slabs)], axis=1), mslab[slot], lslab[slot])

    groups = [token_major(o_ref, m_ref, l_ref, d, slot) for slot, (o_ref, m_ref, l_ref, (_, d)) in
              enumerate(zip((o0_ref, o1_ref, o2_ref), (m0_ref, m1_ref, m2_ref),
                            (l0_ref, l1_ref, l2_ref), DIL_PAIRS))]
    ex = ex_ref[...]

    def spread(wt):
        return jnp.dot(wt.astype(BF16), ex, preferred_element_type=F32)

    mx = jnp.maximum(jnp.maximum(groups[0][1], groups[1][1]), groups[2][1])
    es = [jnp.exp2(m - mx) for _, m, _ in groups]
    inv = 1.0 / (es[0] * groups[0][2] + es[1] * groups[1][2] + es[2] * groups[2][2])
    a_out = (spread(es[0] * inv) * groups[0][0] + spread(es[1] * inv) * groups[1][0]
             + spread(es[2] * inv) * groups[2][0])
    a_gate = g_ref[:, G_A_GATE:G_A_GATE + WIDTH_A]
    b_gate = g_ref[:, G_B_GATE:G_B_GATE + WIDTH_B]
    a_in = (a_out * (a_gate * _sigmoid(a_gate)).astype(F32)).astype(BF16)
    b_in = bo_ref[...] * (b_gate * _sigmoid(b_gate))
    a_y = jnp.dot(a_in, wa_ref[...], preferred_element_type=F32)
    b_y = jnp.dot(b_in, wb_ref[...], preferred_element_type=F32)
    merged = (_sigmoid(g_ref[:, G_MG_A:G_MG_A + D_MODEL]).astype(F32) * a_y
              + _sigmoid(g_ref[:, G_MG_B:G_MG_B + D_MODEL]).astype(F32) * b_y)
    x1 = x_ref[...] + jnp.dot(merged.astype(BF16), wo_ref[...], preferred_element_type=F32)
    ms = jnp.mean(x1 * x1, axis=-1, keepdims=True)
    xn = (x1 * lax.rsqrt(ms + EPS) * np_ref[...]).astype(BF16)
    gate = _sigmoid(jnp.dot(xn, wg_ref[...], preferred_element_type=F32))
    ple = jnp.dot(p_ref[...].astype(BF16), wp_ref[...], preferred_element_type=F32)
    y_ref[...] = x1 + gate * ple


def _merge(x3, p4, layer, outs, bo, g, wa, wb, wo, wg, wp, npl, ex, tile):
    b, s, _ = x3.shape
    row = lambda bi, i: (bi, i, 0)
    const = lambda bi, i: (0, 0)

    def rows(width):
        return pl.BlockSpec((None, tile, width), row)

    def grouped(width, d):
        return pl.BlockSpec((None, d, tile // d, width), lambda bi, i: (bi, 0, i, 0))

    def whole(r, c):
        return pl.BlockSpec((r, c), const, pipeline_mode=pl.Buffered(1))

    dils = [d for _, d in DIL_PAIRS]
    return pl.pallas_call(
        _merge_kernel,
        grid=(b, s // tile),
        in_specs=[rows(D_MODEL), pl.BlockSpec((None, None, tile, PLE_DIM), lambda bi, i: (layer, bi, i, 0))]
        + [grouped(WIDTH_A, d) for d in dils] + 2 * [grouped(LANES, d) for d in dils]
        + [rows(WIDTH_B), rows(G_COLS),
           whole(WIDTH_A, D_MODEL), whole(WIDTH_B, D_MODEL), whole(D_MODEL, D_MODEL),
           whole(D_MODEL, D_MODEL), whole(PLE_DIM, D_MODEL), whole(1, D_MODEL),
           whole(LANES, WIDTH_A)],
        out_specs=rows(D_MODEL),
        out_shape=jax.ShapeDtypeStruct((b, s, D_MODEL), F32),
        scratch_shapes=[
            pltpu.VMEM((N_GROUPS_A, WIDTH_A // LANES, tile, LANES), F32),
            pltpu.VMEM((N_GROUPS_A, tile, LANES), F32),
            pltpu.VMEM((N_GROUPS_A, tile, LANES), F32),
        ],
        compiler_params=pltpu.CompilerParams(
            dimension_semantics=("arbitrary", "arbitrary"), vmem_limit_bytes=VMEM_LIMIT),
        name="merge_out",
    )(x3, p4, *[o for o, _, _ in outs], *[m for _, m, _ in outs], *[l for _, _, l in outs],
      bo, g, wa, wb, wo, wg, wp, npl, ex)


def _rope_tables(s):
    inv = ROPE_THETA ** (-jnp.arange(0, ROT_DIM, 2, dtype=F32) / ROT_DIM)
    ang = jnp.arange(s, dtype=F32)[:, None] * inv[None, :]
    ang = jnp.concatenate([ang, ang], axis=-1)
    cos = jnp.cos(ang)
    sin = jnp.sin(ang)
    half = ROT_DIM // 2
    pad = HEAD_DIM - ROT_DIM
    c_head = jnp.concatenate([cos, jnp.ones((s, pad), F32)], axis=1)
    s1_head = jnp.concatenate([-sin[:, :half], jnp.zeros((s, HEAD_DIM - half), F32)], axis=1)
    s2_head = jnp.concatenate([jnp.zeros((s, half), F32), sin[:, half:], jnp.zeros((s, pad), F32)], axis=1)
    reps = LANES // HEAD_DIM
    return jnp.tile(c_head, (1, reps)), jnp.tile(s1_head, (1, reps)), jnp.tile(s2_head, (1, reps))


def _layer_constants():
    lane = jnp.arange(SLAB)
    ones = ((lane[:, None] // HEAD_DIM == lane[None, :] // HEAD_DIM) * (1.0 / HEAD_DIM)).astype(BF16)
    spread = (jnp.arange(LANES)[:, None] == jnp.arange(WIDTH_A)[None, :] // HEAD_DIM).astype(BF16)
    return ones, spread


def _trunk(x, p, norm_mix, w_in, a_q_norm, a_k_norm, b_q_norm, b_k_norm, b_sink, w_branch_a,
           w_branch_b, w_out, norm_ple, w_ple, w_ple_gate):
    b, s, _ = x.shape
    depth = w_in.shape[0]
    row_tile = min(ROW_TILE, s)
    c_t, s1_t, s2_t = _rope_tables(s)
    ones, spread = _layer_constants()
    scale = HEAD_DIM ** -0.5 * LOG2E
    heads = WIDTH_A // HEAD_DIM
    for i in range(depth):
        ga = jnp.concatenate([jnp.tile(a_q_norm[i] * scale, (1, heads)).reshape(1, -1),
                              jnp.tile(a_k_norm[i], (1, heads)).reshape(1, -1)], axis=1)
        gb = jnp.concatenate([jnp.tile(b_q_norm[i] * scale, N_HEADS_B),
                              jnp.tile(b_k_norm[i], KV_WIDTH_B // HEAD_DIM)])[None, :]
        qa0, qa1, qa2, qb, g = _project(x, norm_mix[i][None, :], w_in[i].astype(BF16), ga, gb,
                                        c_t, s1_t, s2_t, ones, row_tile)
        outs = [_attention_a(qkv, window, ATTN_TILE)
                for qkv, (window, _) in zip((qa0, qa1, qa2), DIL_PAIRS)]
        bo = _attention_b(qb, b_sink[i], ATTN_TILE)
        x = _merge(x, p, i, outs, bo, g,
                   w_branch_a[i].astype(BF16), w_branch_b[i].astype(BF16), w_out[i].astype(BF16),
                   w_ple_gate[i].astype(BF16), w_ple[i].astype(BF16), norm_ple[i][None, :],
                   spread, row_tile)
    return x


def kernel(x_prompt, x_sample, p_prompt, p_sample, norm_mix, w_in, a_q_norm, a_k_norm, b_q_norm,
           b_k_norm, b_sink, w_branch_a, w_branch_b, w_out, norm_ple, w_ple, w_ple_gate):
    weights = (norm_mix, w_in, a_q_norm, a_k_norm, b_q_norm, b_k_norm, b_sink, w_branch_a,
               w_branch_b, w_out, norm_ple, w_ple, w_ple_gate)
    return (_trunk(x_prompt, p_prompt, *weights), _trunk(x_sample, p_sample, *weights))
```

```python
import functools

import jax
import jax.numpy as jnp
from jax import lax
from jax.experimental import pallas as pl
from jax.experimental.pallas import tpu as pltpu

D_MODEL = 1024
HEAD_DIM = 64
ROT_DIM = HEAD_DIM // 4
ROPE_THETA = 500000.0
DIL_PAIRS = ((128, 1), (512, 4), (2048, 16))
N_GROUPS_A = 3
WIDTH_A = 512
WIDTH_B = 512
KV_WIDTH_B = 128
N_HEADS_B = 8
WINDOW_B = 128
PLE_DIM = 256
A_QKV_COLS = 3 * N_GROUPS_A * WIDTH_A
IN_COLS = A_QKV_COLS + WIDTH_A + WIDTH_B + 2 * KV_WIDTH_B + WIDTH_B + 2 * D_MODEL
EPS = 1e-6
NEG = -1e30
LOG2E = 1.4426950408889634

OFF_A_GATE = A_QKV_COLS
OFF_BQ = OFF_A_GATE + WIDTH_A
OFF_BK = OFF_BQ + WIDTH_B
OFF_BV = OFF_BK + KV_WIDTH_B
OFF_B_GATE = OFF_BV + KV_WIDTH_B
OFF_MG_A = OFF_B_GATE + WIDTH_B
OFF_MG_B = OFF_MG_A + D_MODEL

G_A_GATE = 0
G_B_GATE = G_A_GATE + WIDTH_A
G_MG_A = G_B_GATE + WIDTH_B
G_MG_B = G_MG_A + D_MODEL
G_COLS = G_MG_B + D_MODEL

LANES = 128
SLAB = 256
HEADS_PER_SLAB = SLAB // HEAD_DIM
Q_BLOCK = 128
ROW_TILE = 512
ATTN_TILE = 2048
VMEM_LIMIT = 56 * 1024 * 1024

BF16 = jnp.bfloat16
F32 = jnp.float32


def _residue_major(ref, dilation):
    rows = ref.shape[0] // dilation
    return jnp.concatenate([ref[pl.ds(r, rows, stride=dilation), :] for r in range(dilation)], axis=0)


def _proj_kernel(x_ref, nm_ref, w_ref, ga_ref, gb_ref, c_ref, s1_ref, s2_ref, ones_ref,
                 qa0_ref, qa1_ref, qa2_ref, qb_ref, g_ref, hslab):
    xf = x_ref[...]
    ms = jnp.mean(xf * xf, axis=-1, keepdims=True)
    hn32 = xf * lax.rsqrt(ms + EPS) * nm_ref[...]
    n_slabs = D_MODEL // LANES
    for j in range(n_slabs):
        hslab[j] = hn32[:, j * LANES:(j + 1) * LANES]
    ones = ones_ref[...]

    def hn_for(dilation):
        if dilation == 1:
            return hn32.astype(BF16)
        return jnp.concatenate([_residue_major(hslab.at[j], dilation) for j in range(n_slabs)],
                               axis=1).astype(BF16)

    def tables_for(dilation):
        if dilation == 1:
            return c_ref[...], s1_ref[...], s2_ref[...]
        return tuple(_residue_major(t, dilation) for t in (c_ref, s1_ref, s2_ref))

    def wide(t):
        return jnp.concatenate([t, t], axis=1)

    def norm_rope(t, gain, cc, sa, sb, ones_m):
        width = t.shape[1]
        ms = jnp.dot((t * t).astype(BF16), ones_m, preferred_element_type=F32)
        n = t * lax.rsqrt(ms + EPS) * gain
        up = pltpu.roll(n, width - ROT_DIM // 2, 1)
        dn = pltpu.roll(n, ROT_DIM // 2, 1)
        return n * cc + up * sa + dn * sb

    def proj(hn, lo, width):
        return jnp.dot(hn, w_ref[:, lo:lo + width], preferred_element_type=F32)

    def normed(hn, lo, gains, glo, tabs):
        t = proj(hn, lo, WIDTH_A)
        cc, sa, sb = (wide(t_) for t_ in tabs)
        halves = [norm_rope(t[:, h * SLAB:(h + 1) * SLAB], gains[:, glo + h * SLAB:glo + (h + 1) * SLAB],
                            cc, sa, sb, ones) for h in range(WIDTH_A // SLAB)]
        return jnp.concatenate(halves, axis=1).astype(BF16)

    for gi, (out_ref, (_, dilation)) in enumerate(zip((qa0_ref, qa1_ref, qa2_ref), DIL_PAIRS)):
        hn = hn_for(dilation)
        tabs = tables_for(dilation)
        q = normed(hn, gi * WIDTH_A, ga_ref, gi * WIDTH_A, tabs)
        k = normed(hn, (N_GROUPS_A + gi) * WIDTH_A, ga_ref, (N_GROUPS_A + gi) * WIDTH_A, tabs)
        v = proj(hn, (2 * N_GROUPS_A + gi) * WIDTH_A, WIDTH_A).astype(BF16)
        rows = out_ref.shape[1]
        for r in range(dilation):
            rs = slice(r * rows, (r + 1) * rows)
            out_ref[r, :, 0:WIDTH_A] = q[rs]
            out_ref[r, :, WIDTH_A:2 * WIDTH_A] = k[rs]
            out_ref[r, :, 2 * WIDTH_A:] = v[rs]

    hn = hn_for(1)
    tabs = tables_for(1)
    qb_ref[:, 0:WIDTH_B] = normed(hn, OFF_BQ, gb_ref, 0, tabs)
    t = proj(hn, OFF_BK, 2 * KV_WIDTH_B)
    kb = norm_rope(t[:, :KV_WIDTH_B], gb_ref[:, WIDTH_B:WIDTH_B + KV_WIDTH_B], tabs[0], tabs[1], tabs[2],
                   ones[:KV_WIDTH_B, :KV_WIDTH_B])
    qb_ref[:, WIDTH_B:WIDTH_B + KV_WIDTH_B] = kb.astype(BF16)
    qb_ref[:, WIDTH_B + KV_WIDTH_B:] = t[:, KV_WIDTH_B:].astype(BF16)
    g_ref[:, G_A_GATE:G_A_GATE + WIDTH_A] = proj(hn, OFF_A_GATE, WIDTH_A).astype(BF16)
    g_ref[:, G_B_GATE:G_B_GATE + WIDTH_B] = proj(hn, OFF_B_GATE, WIDTH_B).astype(BF16)
    for j in range(2 * D_MODEL // WIDTH_A):
        g_ref[:, G_MG_A + j * WIDTH_A:G_MG_A + (j + 1) * WIDTH_A] = proj(
            hn, OFF_MG_A + j * WIDTH_A, WIDTH_A).astype(BF16)


def _project(x3, nm, w, ga, gb, c_t, s1_t, s2_t, ones, tile):
    b, s, _ = x3.shape
    const = lambda bi, i: (0, 0)
    tab = lambda bi, i: (i, 0)
    row = lambda bi, i: (bi, i, 0)
    qa_specs, qa_shapes = [], []
    for _, d in DIL_PAIRS:
        qa_specs.append(pl.BlockSpec((None, d, tile // d, 3 * WIDTH_A), lambda bi, i: (bi, 0, i, 0)))
        qa_shapes.append(jax.ShapeDtypeStruct((b, d, s // d, 3 * WIDTH_A), BF16))
    return pl.pallas_call(
        _proj_kernel,
        grid=(b, s // tile),
        in_specs=[
            pl.BlockSpec((None, tile, D_MODEL), row),
            pl.BlockSpec((1, D_MODEL), const),
            pl.BlockSpec((D_MODEL, IN_COLS), const, pipeline_mode=pl.Buffered(1)),
            pl.BlockSpec((1, 2 * N_GROUPS_A * WIDTH_A), const),
            pl.BlockSpec((1, WIDTH_B + KV_WIDTH_B), const),
            pl.BlockSpec((tile, LANES), tab),
            pl.BlockSpec((tile, LANES), tab),
            pl.BlockSpec((tile, LANES), tab),
            pl.BlockSpec((SLAB, SLAB), const),
        ],
        out_specs=qa_specs + [
            pl.BlockSpec((None, tile, WIDTH_B + 2 * KV_WIDTH_B), row),
            pl.BlockSpec((None, tile, G_COLS), row),
        ],
        out_shape=qa_shapes + [
            jax.ShapeDtypeStruct((b, s, WIDTH_B + 2 * KV_WIDTH_B), BF16),
            jax.ShapeDtypeStruct((b, s, G_COLS), BF16),
        ],
        scratch_shapes=[pltpu.VMEM((D_MODEL // LANES, tile, LANES), F32)],
        compiler_params=pltpu.CompilerParams(
            dimension_semantics=("arbitrary", "arbitrary"), vmem_limit_bytes=VMEM_LIMIT),
        name="proj_in",
    )(x3, nm, w, ga, gb, c_t, s1_t, s2_t, ones)


def _band_bias(base, half_window, n_keys, length):
    qi = lax.broadcasted_iota(jnp.int32, (Q_BLOCK, 1), 0)
    jj = lax.broadcasted_iota(jnp.int32, (Q_BLOCK, n_keys), 1)
    lo, hi = qi, qi + 2 * half_window
    if base is not None:
        lo = jnp.maximum(lo, half_window - base)
        hi = jnp.minimum(hi, length - 1 - base + half_window)
    return jnp.where(jj >= lo, jnp.where(jj <= hi, 0.0, NEG), NEG).astype(F32)


def _block_biases(t0, n_blocks, half_window, n_keys, length):
    inner = _band_bias(None, half_window, n_keys, length) if n_blocks > 2 else None
    return [_band_bias(t0 + i * Q_BLOCK, half_window, n_keys, length) if i in (0, n_blocks - 1) else inner
            for i in range(n_blocks)]


def _window(prev_ref, main_ref, next_ref, start, n, cols):
    w = prev_ref.shape[0]
    tile = main_ref.shape[0]
    pieces = []
    if start < 0:
        pieces.append(prev_ref[w + start:w, cols])
    pieces.append(main_ref[max(start, 0):min(start + n, tile), cols])
    if start + n > tile:
        pieces.append(next_ref[0:start + n - tile, cols])
    return pieces[0] if len(pieces) == 1 else jnp.concatenate(pieces, axis=0)


def _slab_attention(q_slab, kw, vw, bias, sinks, normalize):
    head_of_lane = lax.broadcasted_iota(jnp.int32, (1, SLAB), 1) // HEAD_DIM
    heads = range(HEADS_PER_SLAB)
    q_bits = pltpu.bitcast(q_slab, jnp.uint32)
    keep = [jnp.where(head_of_lane == h, jnp.uint32(0xFFFFFFFF), jnp.uint32(0)) for h in heads]
    q4 = jnp.concatenate([pltpu.bitcast(q_bits & keep[h], BF16) for h in heads], axis=0)
    s4 = lax.dot_general(q4, kw, (((1,), (1,)), ((), ())), preferred_element_type=F32)
    probs, ms, ls = [], [], []
    for h in heads:
        s = s4[h * Q_BLOCK:(h + 1) * Q_BLOCK] + bias
        m = jnp.max(s, axis=-1, keepdims=True)
        e = jnp.exp2(s - m)
        l = jnp.sum(e, axis=-1, keepdims=True)
        if sinks is not None:
            l = l + jnp.exp2(sinks[h] - m)
        probs.append(e.astype(BF16))
        ms.append(m)
        ls.append(l)
    acc4 = jnp.dot(jnp.concatenate(probs, axis=0), vw, preferred_element_type=F32)
    first_head = lax.broadcasted_iota(jnp.int32, (1, LANES), 1) < HEAD_DIM
    columns = []
    for c in range(SLAB // LANES):
        pair = []
        for h in (2 * c, 2 * c + 1):
            part = acc4[h * Q_BLOCK:(h + 1) * Q_BLOCK, c * LANES:(c + 1) * LANES]
            pair.append(part * (1.0 / ls[h]) if normalize else part)
        columns.append(jnp.where(first_head, pair[0], pair[1]))
    return jnp.concatenate(columns, axis=1), ms, ls


def _attn_a_kernel(q_ref, km_ref, kp_ref, kn_ref, vm_ref, vp_ref, vn_ref, o_ref, m_ref, l_ref,
                   *, half_window, tile, length):
    w = half_window
    n_keys = Q_BLOCK + 2 * w
    n_blocks = tile // Q_BLOCK
    biases = _block_biases(pl.program_id(2) * tile, n_blocks, w, n_keys, length)
    lane = lax.broadcasted_iota(jnp.int32, (1, LANES), 1)
    for rr in range(q_ref.shape[0]):
        k_refs = (kp_ref.at[rr], km_ref.at[rr], kn_ref.at[rr])
        v_refs = (vp_ref.at[rr], vm_ref.at[rr], vn_ref.at[rr])
        for i in range(n_blocks):
            r0 = i * Q_BLOCK
            m_rows = jnp.zeros((Q_BLOCK, LANES), F32)
            l_rows = jnp.ones((Q_BLOCK, LANES), F32)
            for c in range(WIDTH_A // SLAB):
                cols = slice(c * SLAB, (c + 1) * SLAB)
                out, ms, ls = _slab_attention(q_ref[rr, r0:r0 + Q_BLOCK, cols],
                                              _window(*k_refs, r0 - w, n_keys, cols),
                                              _window(*v_refs, r0 - w, n_keys, cols),
                                              biases[i], None, normalize=False)
                o_ref[rr, r0:r0 + Q_BLOCK, cols] = out.astype(BF16)
                for h in range(HEADS_PER_SLAB):
                    m_rows = jnp.where(lane == c * HEADS_PER_SLAB + h, ms[h], m_rows)
                    l_rows = jnp.where(lane == c * HEADS_PER_SLAB + h, ls[h], l_rows)
            m_ref[rr, r0:r0 + Q_BLOCK, :] = m_rows
            l_ref[rr, r0:r0 + Q_BLOCK, :] = l_rows


def _attention_a(qkv, window, rows_per_step):
    b, dilation, length, _ = qkv.shape
    half_window = (window // 2) // dilation
    tile = min(rows_per_step, length)
    residues = min(dilation, rows_per_step // tile)
    halo_per_tile = tile // half_window
    n_halo = length // half_window

    def main(sec):
        return pl.BlockSpec((None, residues, tile, WIDTH_A), lambda bi, r, l: (bi, r, l, sec))

    def prev(sec):
        return pl.BlockSpec((None, residues, half_window, WIDTH_A),
                            lambda bi, r, l: (bi, r, jnp.maximum(l * halo_per_tile - 1, 0), sec))

    def nxt(sec):
        return pl.BlockSpec((None, residues, half_window, WIDTH_A),
                            lambda bi, r, l: (bi, r, jnp.minimum((l + 1) * halo_per_tile, n_halo - 1), sec))

    return pl.pallas_call(
        functools.partial(_attn_a_kernel, half_window=half_window, tile=tile, length=length),
        grid=(b, dilation // residues, length // tile),
        in_specs=[main(0), main(1), prev(1), nxt(1), main(2), prev(2), nxt(2)],
        out_specs=[
            pl.BlockSpec((None, residues, tile, WIDTH_A), lambda bi, r, l: (bi, r, l, 0)),
            pl.BlockSpec((None, residues, tile, LANES), lambda bi, r, l: (bi, r, l, 0)),
            pl.BlockSpec((None, residues, tile, LANES), lambda bi, r, l: (bi, r, l, 0)),
        ],
        out_shape=[
            jax.ShapeDtypeStruct((b, dilation, length, WIDTH_A), BF16),
            jax.ShapeDtypeStruct((b, dilation, length, LANES), F32),
            jax.ShapeDtypeStruct((b, dilation, length, LANES), F32),
        ],
        compiler_params=pltpu.CompilerParams(
            dimension_semantics=("arbitrary", "arbitrary", "arbitrary"),
            vmem_limit_bytes=VMEM_LIMIT),
        name=f"attn_a_d{dilation}",
    )(qkv, qkv, qkv, qkv, qkv, qkv, qkv)


def _attn_b_kernel(sink_ref, q_ref, km_ref, kp_ref, kn_ref, vm_ref, vp_ref, vn_ref, o_ref,
                   *, tile, length):
    w = WINDOW_B
    n_keys = Q_BLOCK + 2 * w
    n_blocks = tile // Q_BLOCK
    biases = _block_biases(pl.program_id(1) * tile, n_blocks, w, n_keys, length)
    even_block = (lax.broadcasted_iota(jnp.int32, (1, SLAB), 1) // HEAD_DIM) % 2 == 0
    all_lanes = slice(0, KV_WIDTH_B)

    def per_kv_head(x):
        swapped = jnp.concatenate([x[:, HEAD_DIM:], x[:, :HEAD_DIM]], axis=1)
        pair = jnp.concatenate([x, x], axis=1)
        riap = jnp.concatenate([swapped, swapped], axis=1)
        return jnp.where(even_block, pair, riap), jnp.where(even_block, riap, pair)

    for i in range(n_blocks):
        r0 = i * Q_BLOCK
        k_heads = per_kv_head(_window(kp_ref, km_ref, kn_ref, r0 - w, n_keys, all_lanes))
        v_heads = per_kv_head(_window(vp_ref, vm_ref, vn_ref, r0 - w, n_keys, all_lanes))
        for c in range(WIDTH_B // SLAB):
            sinks = [sink_ref[c * HEADS_PER_SLAB + h] * LOG2E for h in range(HEADS_PER_SLAB)]
            cols = slice(c * SLAB, (c + 1) * SLAB)
            out, _, _ = _slab_attention(q_ref[r0:r0 + Q_BLOCK, cols], k_heads[c], v_heads[c],
                                        biases[i], sinks, normalize=True)
            o_ref[r0:r0 + Q_BLOCK, cols] = out.astype(BF16)


def _attention_b(qb3, sink, tile):
    b, s, _ = qb3.shape
    tile = min(tile, s)
    halo_per_tile = tile // WINDOW_B
    n_halo = s // WINDOW_B
    k_col = WIDTH_B // KV_WIDTH_B
    v_col = k_col + 1

    def main(col):
        return pl.BlockSpec((None, tile, KV_WIDTH_B), lambda bi, l, sk: (bi, l, col))

    def prev(col):
        return pl.BlockSpec((None, WINDOW_B, KV_WIDTH_B),
                            lambda bi, l, sk: (bi, jnp.maximum(l * halo_per_tile - 1, 0), col))

    def nxt(col):
        return pl.BlockSpec((None, WINDOW_B, KV_WIDTH_B),
                            lambda bi, l, sk: (bi, jnp.minimum((l + 1) * halo_per_tile, n_halo - 1), col))

    return pl.pallas_call(
        functools.partial(_attn_b_kernel, tile=tile, length=s),
        grid_spec=pltpu.PrefetchScalarGridSpec(
            num_scalar_prefetch=1,
            grid=(b, s // tile),
            in_specs=[
                pl.BlockSpec((None, tile, WIDTH_B), lambda bi, l, sk: (bi, l, 0)),
                main(k_col), prev(k_col), nxt(k_col), main(v_col), prev(v_col), nxt(v_col),
            ],
            out_specs=pl.BlockSpec((None, tile, WIDTH_B), lambda bi, l, sk: (bi, l, 0)),
        ),
        out_shape=jax.ShapeDtypeStruct((b, s, WIDTH_B), BF16),
        compiler_params=pltpu.CompilerParams(
            dimension_semantics=("arbitrary", "arbitrary"), vmem_limit_bytes=VMEM_LIMIT),
        name="attn_b",
    )(sink, qb3, qb3, qb3, qb3, qb3, qb3, qb3)


def _sigmoid(x):
    half = jnp.asarray(0.5, x.dtype)
    return half * jnp.tanh(half * x) + half


def _merge_kernel(x_ref, p_ref, o0_ref, o1_ref, o2_ref, m0_ref, m1_ref, m2_ref, l0_ref, l1_ref, l2_ref,
                  bo_ref, g_ref, wa_ref, wb_ref, wo_ref, wg_ref, wp_ref, np_ref, ex_ref, y_ref,
                  oslab, mslab, lslab):
    n_slabs = WIDTH_A // LANES

    def token_major(o_ref, m_ref, l_ref, dilation, slot):
        if dilation == 1:
            return o_ref[0].astype(F32), m_ref[0], l_ref[0]
        rows = o_ref.shape[1]
        for r in range(dilation):
            o_r = o_ref[r].astype(F32)
            for j in range(n_slabs):
                oslab[slot, j, pl.ds(r, rows, stride=dilation), :] = o_r[:, j * LANES:(j + 1) * LANES]
            mslab[slot, pl.ds(r, rows, stride=dilation), :] = m_ref[r]
            lslab[slot, pl.ds(r, rows, stride=dilation), :] = l_ref[r]
        return (jnp.concatenate([oslab[slot, j] for j in range(n_slabs)], axis=1), mslab[slot], lslab[slot])

    groups = [token_major(o_ref, m_ref, l_ref, d, slot) for slot, (o_ref, m_ref, l_ref, (_, d)) in
              enumerate(zip((o0_ref, o1_ref, o2_ref), (m0_ref, m1_ref, m2_ref),
                            (l0_ref, l1_ref, l2_ref), DIL_PAIRS))]
    ex = ex_ref[...]

    def spread(wt):
        return jnp.dot(wt.astype(BF16), ex, preferred_element_type=F32)

    mx = jnp.maximum(jnp.maximum(groups[0][1], groups[1][1]), groups[2][1])
    es = [jnp.exp2(m - mx) for _, m, _ in groups]
    inv = 1.0 / (es[0] * groups[0][2] + es[1] * groups[1][2] + es[2] * groups[2][2])
    a_out = (spread(es[0] * inv) * groups[0][0] + spread(es[1] * inv) * groups[1][0]
             + spread(es[2] * inv) * groups[2][0])
    a_gate = g_ref[:, G_A_GATE:G_A_GATE + WIDTH_A]
    b_gate = g_ref[:, G_B_GATE:G_B_GATE + WIDTH_B]
    a_in = (a_out * (a_gate * _sigmoid(a_gate)).astype(F32)).astype(BF16)
    b_in = bo_ref[...] * (b_gate * _sigmoid(b_gate))
    a_y = jnp.dot(a_in, wa_ref[...], preferred_element_type=F32)
    b_y = jnp.dot(b_in, wb_ref[...], preferred_element_type=F32)
    merged = (_sigmoid(g_ref[:, G_MG_A:G_MG_A + D_MODEL]).astype(F32) * a_y
              + _sigmoid(g_ref[:, G_MG_B:G_MG_B + D_MODEL]).astype(F32) * b_y)
    x1 = x_ref[...] + jnp.dot(merged.astype(BF16), wo_ref[...], preferred_element_type=F32)
    ms = jnp.mean(x1 * x1, axis=-1, keepdims=True)
    xn = (x1 * lax.rsqrt(ms + EPS) * np_ref[...]).astype(BF16)
    gate = _sigmoid(jnp.dot(xn, wg_ref[...], preferred_element_type=F32))
    ple = jnp.dot(p_ref[...].astype(BF16), wp_ref[...], preferred_element_type=F32)
    y_ref[...] = x1 + gate * ple


def _merge(x3, p4, layer, outs, bo, g, wa, wb, wo, wg, wp, npl, ex, tile):
    b, s, _ = x3.shape
    row = lambda bi, i: (bi, i, 0)
    const = lambda bi, i: (0, 0)

    def rows(width):
        return pl.BlockSpec((None, tile, width), row)

    def grouped(width, d):
        return pl.BlockSpec((None, d, tile // d, width), lambda bi, i: (bi, 0, i, 0))

    def whole(r, c):
        return pl.BlockSpec((r, c), const, pipeline_mode=pl.Buffered(1))

    dils = [d for _, d in DIL_PAIRS]
    return pl.pallas_call(
        _merge_kernel,
        grid=(b, s // tile),
        in_specs=[rows(D_MODEL), pl.BlockSpec((None, None, tile, PLE_DIM), lambda bi, i: (layer, bi, i, 0))]
        + [grouped(WIDTH_A, d) for d in dils] + 2 * [grouped(LANES, d) for d in dils]
        + [rows(WIDTH_B), rows(G_COLS),
           whole(WIDTH_A, D_MODEL), whole(WIDTH_B, D_MODEL), whole(D_MODEL, D_MODEL),
           whole(D_MODEL, D_MODEL), whole(PLE_DIM, D_MODEL), whole(1, D_MODEL),
           whole(LANES, WIDTH_A)],
        out_specs=rows(D_MODEL),
        out_shape=jax.ShapeDtypeStruct((b, s, D_MODEL), F32),
        scratch_shapes=[
            pltpu.VMEM((N_GROUPS_A, WIDTH_A // LANES, tile, LANES), F32),
            pltpu.VMEM((N_GROUPS_A, tile, LANES), F32),
            pltpu.VMEM((N_GROUPS_A, tile, LANES), F32),
        ],
        compiler_params=pltpu.CompilerParams(
            dimension_semantics=("arbitrary", "arbitrary"), vmem_limit_bytes=VMEM_LIMIT),
        name="merge_out",
    )(x3, p4, *[o for o, _, _ in outs], *[m for _, m, _ in outs], *[l for _, _, l in outs],
      bo, g, wa, wb, wo, wg, wp, npl, ex)


def _rope_tables(s):
    inv = ROPE_THETA ** (-jnp.arange(0, ROT_DIM, 2, dtype=F32) / ROT_DIM)
    ang = jnp.arange(s, dtype=F32)[:, None] * inv[None, :]
    ang = jnp.concatenate([ang, ang], axis=-1)
    cos = jnp.cos(ang)
    sin = jnp.sin(ang)
    half = ROT_DIM // 2
    pad = HEAD_DIM - ROT_DIM
    c_head = jnp.concatenate([cos, jnp.ones((s, pad), F32)], axis=1)
    s1_head = jnp.concatenate([-sin[:, :half], jnp.zeros((s, HEAD_DIM - half), F32)], axis=1)
    s2_head = jnp.concatenate([jnp.zeros((s, half), F32), sin[:, half:], jnp.zeros((s, pad), F32)], axis=1)
    reps = LANES // HEAD_DIM
    return jnp.tile(c_head, (1, reps)), jnp.tile(s1_head, (1, reps)), jnp.tile(s2_head, (1, reps))


def _layer_constants():
    lane = jnp.arange(SLAB)
    ones = ((lane[:, None] // HEAD_DIM == lane[None, :] // HEAD_DIM) * (1.0 / HEAD_DIM)).astype(BF16)
    spread = (jnp.arange(LANES)[:, None] == jnp.arange(WIDTH_A)[None, :] // HEAD_DIM).astype(BF16)
    return ones, spread


def _trunk(x, p, norm_mix, w_in, a_q_norm, a_k_norm, b_q_norm, b_k_norm, b_sink, w_branch_a,
           w_branch_b, w_out, norm_ple, w_ple, w_ple_gate):
    b, s, _ = x.shape
    depth = w_in.shape[0]
    row_tile = min(ROW_TILE, s)
    c_t, s1_t, s2_t = _rope_tables(s)
    ones, spread = _layer_constants()
    scale = HEAD_DIM ** -0.5 * LOG2E
    heads = WIDTH_A // HEAD_DIM
    for i in range(depth):
        ga = jnp.concatenate([jnp.tile(a_q_norm[i] * scale, (1, heads)).reshape(1, -1),
                              jnp.tile(a_k_norm[i], (1, heads)).reshape(1, -1)], axis=1)
        gb = jnp.concatenate([jnp.tile(b_q_norm[i] * scale, N_HEADS_B),
                              jnp.tile(b_k_norm[i], KV_WIDTH_B // HEAD_DIM)])[None, :]
        qa0, qa1, qa2, qb, g = _project(x, norm_mix[i][None, :], w_in[i].astype(BF16), ga, gb,
                                        c_t, s1_t, s2_t, ones, row_tile)
        outs = [_attention_a(qkv, window, ATTN_TILE)
                for qkv, (window, _) in zip((qa0, qa1, qa2), DIL_PAIRS)]
        bo = _attention_b(qb, b_sink[i], ATTN_TILE)
        x = _merge(x, p, i, outs, bo, g,
                   w_branch_a[i].astype(BF16), w_branch_b[i].astype(BF16), w_out[i].astype(BF16),
                   w_ple_gate[i].astype(BF16), w_ple[i].astype(BF16), norm_ple[i][None, :],
                   spread, row_tile)
    return x


def kernel(x_prompt, x_sample, p_prompt, p_sample, norm_mix, w_in, a_q_norm, a_k_norm, b_q_norm,
           b_k_norm, b_sink, w_branch_a, w_branch_b, w_out, norm_ple, w_ple, w_ple_gate):
    weights = (norm_mix, w_in, a_q_norm, a_k_norm, b_q_norm, b_k_norm, b_sink, w_branch_a,
               w_branch_b, w_out, norm_ple, w_ple, w_ple_gate)
    return (_trunk(x_prompt, p_prompt, *weights), _trunk(x_sample, p_sample, *weights))
```

```python
import functools

import jax
import jax.numpy as jnp
from jax import lax
from jax.experimental import pallas as pl
from jax.experimental.pallas import tpu as pltpu

D_MODEL = 1024
HEAD_DIM = 64
ROT_DIM = HEAD_DIM // 4
ROPE_THETA = 500000.0
DIL_PAIRS = ((128, 1), (512, 4), (2048, 16))
N_GROUPS_A = 3
WIDTH_A = 512
WIDTH_B = 512
KV_WIDTH_B = 128
N_HEADS_B = 8
WINDOW_B = 128
PLE_DIM = 256
A_QKV_COLS = 3 * N_GROUPS_A * WIDTH_A
IN_COLS = A_QKV_COLS + WIDTH_A + WIDTH_B + 2 * KV_WIDTH_B + WIDTH_B + 2 * D_MODEL
EPS = 1e-6
NEG = -1e30
LOG2E = 1.4426950408889634

OFF_A_GATE = A_QKV_COLS
OFF_BQ = OFF_A_GATE + WIDTH_A
OFF_BK = OFF_BQ + WIDTH_B
OFF_BV = OFF_BK + KV_WIDTH_B
OFF_B_GATE = OFF_BV + KV_WIDTH_B
OFF_MG_A = OFF_B_GATE + WIDTH_B
OFF_MG_B = OFF_MG_A + D_MODEL

G_A_GATE = 0
G_B_GATE = G_A_GATE + WIDTH_A
G_MG_A = G_B_GATE + WIDTH_B
G_MG_B = G_MG_A + D_MODEL
G_COLS = G_MG_B + D_MODEL

LANES = 128
SLAB = 256
HEADS_PER_SLAB = SLAB // HEAD_DIM
Q_BLOCK = 128
ROW_TILE = 512
ATTN_ROWS_A = 4096
ATTN_ROWS_B = 2048
VMEM_LIMIT = 56 * 1024 * 1024

BF16 = jnp.bfloat16
F32 = jnp.float32


def _residue_major(ref, dilation):
    rows = ref.shape[0] // dilation
    return jnp.concatenate([ref[pl.ds(r, rows, stride=dilation), :] for r in range(dilation)], axis=0)


def _proj_kernel(x_ref, nm_ref, w_ref, ga_ref, gb_ref, c_ref, s1_ref, s2_ref, ones_ref,
                 qa0_ref, qa1_ref, qa2_ref, qb_ref, g_ref, hslab):
    xf = x_ref[...]
    ms = jnp.mean(xf * xf, axis=-1, keepdims=True)
    hn32 = xf * lax.rsqrt(ms + EPS) * nm_ref[...]
    n_slabs = D_MODEL // LANES
    for j in range(n_slabs):
        hslab[j] = hn32[:, j * LANES:(j + 1) * LANES]
    ones = ones_ref[...]

    def hn_for(dilation):
        if dilation == 1:
            return hn32.astype(BF16)
        return jnp.concatenate([_residue_major(hslab.at[j], dilation) for j in range(n_slabs)],
                               axis=1).astype(BF16)

    def tables_for(dilation):
        if dilation == 1:
            return c_ref[...], s1_ref[...], s2_ref[...]
        return tuple(_residue_major(t, dilation) for t in (c_ref, s1_ref, s2_ref))

    def wide(t):
        return jnp.concatenate([t, t], axis=1)

    def norm_rope(t, gain, cc, sa, sb, ones_m):
        width = t.shape[1]
        ms = jnp.dot((t * t).astype(BF16), ones_m, preferred_element_type=F32)
        n = t * lax.rsqrt(ms + EPS) * gain
        up = pltpu.roll(n, width - ROT_DIM // 2, 1)
        dn = pltpu.roll(n, ROT_DIM // 2, 1)
        return n * cc + up * sa + dn * sb

    def proj(hn, lo, width):
        return jnp.dot(hn, w_ref[:, lo:lo + width], preferred_element_type=F32)

    def normed(hn, lo, gains, glo, tabs):
        t = proj(hn, lo, WIDTH_A)
        cc, sa, sb = (wide(t_) for t_ in tabs)
        halves = [norm_rope(t[:, h * SLAB:(h + 1) * SLAB], gains[:, glo + h * SLAB:glo + (h + 1) * SLAB],
                            cc, sa, sb, ones) for h in range(WIDTH_A // SLAB)]
        return jnp.concatenate(halves, axis=1).astype(BF16)

    for gi, (out_ref, (_, dilation)) in enumerate(zip((qa0_ref, qa1_ref, qa2_ref), DIL_PAIRS)):
        hn = hn_for(dilation)
        tabs = tables_for(dilation)
        q = normed(hn, gi * WIDTH_A, ga_ref, gi * WIDTH_A, tabs)
        k = normed(hn, (N_GROUPS_A + gi) * WIDTH_A, ga_ref, (N_GROUPS_A + gi) * WIDTH_A, tabs)
        v = proj(hn, (2 * N_GROUPS_A + gi) * WIDTH_A, WIDTH_A).astype(BF16)
        rows = out_ref.shape[1]
        for r in range(dilation):
            rs = slice(r * rows, (r + 1) * rows)
            out_ref[r, :, 0:WIDTH_A] = q[rs]
            out_ref[r, :, WIDTH_A:2 * WIDTH_A] = k[rs]
            out_ref[r, :, 2 * WIDTH_A:] = v[rs]

    hn = hn_for(1)
    tabs = tables_for(1)
    qb_ref[:, 0:WIDTH_B] = normed(hn, OFF_BQ, gb_ref, 0, tabs)
    t = proj(hn, OFF_BK, 2 * KV_WIDTH_B)
    kb = norm_rope(t[:, :KV_WIDTH_B], gb_ref[:, WIDTH_B:WIDTH_B + KV_WIDTH_B], tabs[0], tabs[1], tabs[2],
                   ones[:KV_WIDTH_B, :KV_WIDTH_B])
    qb_ref[:, WIDTH_B:WIDTH_B + KV_WIDTH_B] = kb.astype(BF16)
    qb_ref[:, WIDTH_B + KV_WIDTH_B:] = t[:, KV_WIDTH_B:].astype(BF16)
    g_ref[:, G_A_GATE:G_A_GATE + WIDTH_A] = proj(hn, OFF_A_GATE, WIDTH_A).astype(BF16)
    g_ref[:, G_B_GATE:G_B_GATE + WIDTH_B] = proj(hn, OFF_B_GATE, WIDTH_B).astype(BF16)
    for j in range(2 * D_MODEL // WIDTH_A):
        g_ref[:, G_MG_A + j * WIDTH_A:G_MG_A + (j + 1) * WIDTH_A] = proj(
            hn, OFF_MG_A + j * WIDTH_A, WIDTH_A).astype(BF16)


def _project(x3, nm, w, ga, gb, c_t, s1_t, s2_t, ones, tile):
    b, s, _ = x3.shape
    const = lambda bi, i: (0, 0)
    tab = lambda bi, i: (i, 0)
    row = lambda bi, i: (bi, i, 0)
    qa_specs, qa_shapes = [], []
    for _, d in DIL_PAIRS:
        qa_specs.append(pl.BlockSpec((None, d, tile // d, 3 * WIDTH_A), lambda bi, i: (bi, 0, i, 0)))
        qa_shapes.append(jax.ShapeDtypeStruct((b, d, s // d, 3 * WIDTH_A), BF16))
    return pl.pallas_call(
        _proj_kernel,
        grid=(b, s // tile),
        in_specs=[
            pl.BlockSpec((None, tile, D_MODEL), row),
            pl.BlockSpec((1, D_MODEL), const),
            pl.BlockSpec((D_MODEL, IN_COLS), const, pipeline_mode=pl.Buffered(1)),
            pl.BlockSpec((1, 2 * N_GROUPS_A * WIDTH_A), const),
            pl.BlockSpec((1, WIDTH_B + KV_WIDTH_B), const),
            pl.BlockSpec((tile, LANES), tab),
            pl.BlockSpec((tile, LANES), tab),
            pl.BlockSpec((tile, LANES), tab),
            pl.BlockSpec((SLAB, SLAB), const),
        ],
        out_specs=qa_specs + [
            pl.BlockSpec((None, tile, WIDTH_B + 2 * KV_WIDTH_B), row),
            pl.BlockSpec((None, tile, G_COLS), row),
        ],
        out_shape=qa_shapes + [
            jax.ShapeDtypeStruct((b, s, WIDTH_B + 2 * KV_WIDTH_B), BF16),
            jax.ShapeDtypeStruct((b, s, G_COLS), BF16),
        ],
        scratch_shapes=[pltpu.VMEM((D_MODEL // LANES, tile, LANES), F32)],
        compiler_params=pltpu.CompilerParams(
            dimension_semantics=("arbitrary", "arbitrary"), vmem_limit_bytes=VMEM_LIMIT),
        name="proj_in",
    )(x3, nm, w, ga, gb, c_t, s1_t, s2_t, ones)


def _band_bias(base, half_window, n_keys, length):
    qi = lax.broadcasted_iota(jnp.int32, (Q_BLOCK, 1), 0)
    jj = lax.broadcasted_iota(jnp.int32, (Q_BLOCK, n_keys), 1)
    lo, hi = qi, qi + 2 * half_window
    if base is not None:
        lo = jnp.maximum(lo, half_window - base)
        hi = jnp.minimum(hi, length - 1 - base + half_window)
    return jnp.where(jj >= lo, jnp.where(jj <= hi, 0.0, NEG), NEG).astype(F32)


def _block_biases(t0, n_blocks, half_window, n_keys, length):
    inner = _band_bias(None, half_window, n_keys, length) if n_blocks > 2 else None
    return [_band_bias(t0 + i * Q_BLOCK, half_window, n_keys, length) if i in (0, n_blocks - 1) else inner
            for i in range(n_blocks)]


def _window(prev_ref, main_ref, next_ref, start, n, cols):
    w = prev_ref.shape[0]
    tile = main_ref.shape[0]
    pieces = []
    if start < 0:
        pieces.append(prev_ref[w + start:w, cols])
    pieces.append(main_ref[max(start, 0):min(start + n, tile), cols])
    if start + n > tile:
        pieces.append(next_ref[0:start + n - tile, cols])
    return pieces[0] if len(pieces) == 1 else jnp.concatenate(pieces, axis=0)


def _slab_attention(q_slab, kw, vw, bias, sinks, normalize):
    head_of_lane = lax.broadcasted_iota(jnp.int32, (1, SLAB), 1) // HEAD_DIM
    heads = range(HEADS_PER_SLAB)
    q_bits = pltpu.bitcast(q_slab, jnp.uint32)
    keep = [jnp.where(head_of_lane == h, jnp.uint32(0xFFFFFFFF), jnp.uint32(0)) for h in heads]
    q4 = jnp.concatenate([pltpu.bitcast(q_bits & keep[h], BF16) for h in heads], axis=0)
    s4 = lax.dot_general(q4, kw, (((1,), (1,)), ((), ())), preferred_element_type=F32)
    probs, ms, ls = [], [], []
    for h in heads:
        s = s4[h * Q_BLOCK:(h + 1) * Q_BLOCK] + bias
        m = jnp.max(s, axis=-1, keepdims=True)
        e = jnp.exp2(s - m)
        l = jnp.sum(e, axis=-1, keepdims=True)
        if sinks is not None:
            l = l + jnp.exp2(sinks[h] - m)
        probs.append(e.astype(BF16))
        ms.append(m)
        ls.append(l)
    acc4 = jnp.dot(jnp.concatenate(probs, axis=0), vw, preferred_element_type=F32)
    first_head = lax.broadcasted_iota(jnp.int32, (1, LANES), 1) < HEAD_DIM
    columns = []
    for c in range(SLAB // LANES):
        pair = []
        for h in (2 * c, 2 * c + 1):
            part = acc4[h * Q_BLOCK:(h + 1) * Q_BLOCK, c * LANES:(c + 1) * LANES]
            pair.append(part * (1.0 / ls[h]) if normalize else part)
        columns.append(jnp.where(first_head, pair[0], pair[1]))
    return jnp.concatenate(columns, axis=1), ms, ls


def _attn_a_kernel(q_ref, km_ref, kp_ref, kn_ref, vm_ref, vp_ref, vn_ref, o_ref, m_ref, l_ref,
                   *, half_window, tile, length):
    w = half_window
    n_keys = Q_BLOCK + 2 * w
    n_blocks = tile // Q_BLOCK
    biases = _block_biases(pl.program_id(2) * tile, n_blocks, w, n_keys, length)
    lane = lax.broadcasted_iota(jnp.int32, (1, LANES), 1)
    for rr in range(q_ref.shape[0]):
        k_refs = (kp_ref.at[rr], km_ref.at[rr], kn_ref.at[rr])
        v_refs = (vp_ref.at[rr], vm_ref.at[rr], vn_ref.at[rr])
        for i in range(n_blocks):
            r0 = i * Q_BLOCK
            m_rows = jnp.zeros((Q_BLOCK, LANES), F32)
            l_rows = jnp.ones((Q_BLOCK, LANES), F32)
            for c in range(WIDTH_A // SLAB):
                cols = slice(c * SLAB, (c + 1) * SLAB)
                out, ms, ls = _slab_attention(q_ref[rr, r0:r0 + Q_BLOCK, cols],
                                              _window(*k_refs, r0 - w, n_keys, cols),
                                              _window(*v_refs, r0 - w, n_keys, cols),
                                              biases[i], None, normalize=False)
                o_ref[rr, r0:r0 + Q_BLOCK, cols] = out.astype(BF16)
                for h in range(HEADS_PER_SLAB):
                    m_rows = jnp.where(lane == c * HEADS_PER_SLAB + h, ms[h], m_rows)
                    l_rows = jnp.where(lane == c * HEADS_PER_SLAB + h, ls[h], l_rows)
            m_ref[rr, r0:r0 + Q_BLOCK, :] = m_rows
            l_ref[rr, r0:r0 + Q_BLOCK, :] = l_rows


def _attention_a(qkv, window, rows_per_step):
    b, dilation, length, _ = qkv.shape
    half_window = (window // 2) // dilation
    tile = min(rows_per_step, length)
    residues = min(dilation, rows_per_step // tile)
    halo_per_tile = tile // half_window
    n_halo = length // half_window

    def main(sec):
        return pl.BlockSpec((None, residues, tile, WIDTH_A), lambda bi, r, l: (bi, r, l, sec))

    def prev(sec):
        return pl.BlockSpec((None, residues, half_window, WIDTH_A),
                            lambda bi, r, l: (bi, r, jnp.maximum(l * halo_per_tile - 1, 0), sec))

    def nxt(sec):
        return pl.BlockSpec((None, residues, half_window, WIDTH_A),
                            lambda bi, r, l: (bi, r, jnp.minimum((l + 1) * halo_per_tile, n_halo - 1), sec))

    return pl.pallas_call(
        functools.partial(_attn_a_kernel, half_window=half_window, tile=tile, length=length),
        grid=(b, dilation // residues, length // tile),
        in_specs=[main(0), main(1), prev(1), nxt(1), main(2), prev(2), nxt(2)],
        out_specs=[
            pl.BlockSpec((None, residues, tile, WIDTH_A), lambda bi, r, l: (bi, r, l, 0)),
            pl.BlockSpec((None, residues, tile, LANES), lambda bi, r, l: (bi, r, l, 0)),
            pl.BlockSpec((None, residues, tile, LANES), lambda bi, r, l: (bi, r, l, 0)),
        ],
        out_shape=[
            jax.ShapeDtypeStruct((b, dilation, length, WIDTH_A), BF16),
            jax.ShapeDtypeStruct((b, dilation, length, LANES), F32),
            jax.ShapeDtypeStruct((b, dilation, length, LANES), F32),
        ],
        compiler_params=pltpu.CompilerParams(
            dimension_semantics=("arbitrary", "arbitrary", "arbitrary"),
            vmem_limit_bytes=VMEM_LIMIT),
        name=f"attn_a_d{dilation}",
    )(qkv, qkv, qkv, qkv, qkv, qkv, qkv)


def _attn_b_kernel(sink_ref, q_ref, km_ref, kp_ref, kn_ref, vm_ref, vp_ref, vn_ref, o_ref,
                   *, tile, length):
    w = WINDOW_B
    n_keys = Q_BLOCK + 2 * w
    n_blocks = tile // Q_BLOCK
    biases = _block_biases(pl.program_id(1) * tile, n_blocks, w, n_keys, length)
    even_block = (lax.broadcasted_iota(jnp.int32, (1, SLAB), 1) // HEAD_DIM) % 2 == 0
    all_lanes = slice(0, KV_WIDTH_B)

    def per_kv_head(x):
        swapped = jnp.concatenate([x[:, HEAD_DIM:], x[:, :HEAD_DIM]], axis=1)
        pair = jnp.concatenate([x, x], axis=1)
        riap = jnp.concatenate([swapped, swapped], axis=1)
        return jnp.where(even_block, pair, riap), jnp.where(even_block, riap, pair)

    k_heads = per_kv_head(_window(kp_ref, km_ref, kn_ref, -w, tile + 2 * w, all_lanes))
    v_heads = per_kv_head(_window(vp_ref, vm_ref, vn_ref, -w, tile + 2 * w, all_lanes))
    for i in range(n_blocks):
        r0 = i * Q_BLOCK
        for c in range(WIDTH_B // SLAB):
            sinks = [sink_ref[c * HEADS_PER_SLAB + h] * LOG2E for h in range(HEADS_PER_SLAB)]
            cols = slice(c * SLAB, (c + 1) * SLAB)
            out, _, _ = _slab_attention(q_ref[r0:r0 + Q_BLOCK, cols], k_heads[c][r0:r0 + n_keys],
                                        v_heads[c][r0:r0 + n_keys], biases[i], sinks, normalize=True)
            o_ref[r0:r0 + Q_BLOCK, cols] = out.astype(BF16)


def _attention_b(qb3, sink, tile):
    b, s, _ = qb3.shape
    tile = min(tile, s)
    halo_per_tile = tile // WINDOW_B
    n_halo = s // WINDOW_B
    k_col = WIDTH_B // KV_WIDTH_B
    v_col = k_col + 1

    def main(col):
        return pl.BlockSpec((None, tile, KV_WIDTH_B), lambda bi, l, sk: (bi, l, col))

    def prev(col):
        return pl.BlockSpec((None, WINDOW_B, KV_WIDTH_B),
                            lambda bi, l, sk: (bi, jnp.maximum(l * halo_per_tile - 1, 0), col))

    def nxt(col):
        return pl.BlockSpec((None, WINDOW_B, KV_WIDTH_B),
                            lambda bi, l, sk: (bi, jnp.minimum((l + 1) * halo_per_tile, n_halo - 1), col))

    return pl.pallas_call(
        functools.partial(_attn_b_kernel, tile=tile, length=s),
        grid_spec=pltpu.PrefetchScalarGridSpec(
            num_scalar_prefetch=1,
            grid=(b, s // tile),
            in_specs=[
                pl.BlockSpec((None, tile, WIDTH_B), lambda bi, l, sk: (bi, l, 0)),
                main(k_col), prev(k_col), nxt(k_col), main(v_col), prev(v_col), nxt(v_col),
            ],
            out_specs=pl.BlockSpec((None, tile, WIDTH_B), lambda bi, l, sk: (bi, l, 0)),
        ),
        out_shape=jax.ShapeDtypeStruct((b, s, WIDTH_B), BF16),
        compiler_params=pltpu.CompilerParams(
            dimension_semantics=("arbitrary", "arbitrary"), vmem_limit_bytes=VMEM_LIMIT),
        name="attn_b",
    )(sink, qb3, qb3, qb3, qb3, qb3, qb3, qb3)


def _sigmoid(x):
    half = jnp.asarray(0.5, x.dtype)
    return half * jnp.tanh(half * x) + half


def _merge_kernel(x_ref, p_ref, o0_ref, o1_ref, o2_ref, m0_ref, m1_ref, m2_ref, l0_ref, l1_ref, l2_ref,
                  bo_ref, g_ref, wa_ref, wb_ref, wo_ref, wg_ref, wp_ref, np_ref, ex_ref, y_ref,
                  oslab, mslab, lslab):
    n_slabs = WIDTH_A // LANES

    def token_major(o_ref, m_ref, l_ref, dilation, slot):
        if dilation == 1:
            return o_ref[0].astype(F32), m_ref[0], l_ref[0]
        rows = o_ref.shape[1]
        for r in range(dilation):
            o_r = o_ref[r].astype(F32)
            for j in range(n_slabs):
                oslab[slot, j, pl.ds(r, rows, stride=dilation), :] = o_r[:, j * LANES:(j + 1) * LANES]
            mslab[slot, pl.ds(r, rows, stride=dilation), :] = m_ref[r]
            lslab[slot, pl.ds(r, rows, stride=dilation), :] = l_ref[r]
        return (jnp.concatenate([oslab[slot, j] for j in range(n_slabs)], axis=1), mslab[slot], lslab[slot])

    groups = [token_major(o_ref, m_ref, l_ref, d, slot) for slot, (o_ref, m_ref, l_ref, (_, d)) in
              enumerate(zip((o0_ref, o1_ref, o2_ref), (m0_ref, m1_ref, m2_ref),
                            (l0_ref, l1_ref, l2_ref), DIL_PAIRS))]
    ex = ex_ref[...]

    def spread(wt):
        return jnp.dot(wt.astype(BF16), ex, preferred_element_type=F32)

    mx = jnp.maximum(jnp.maximum(groups[0][1], groups[1][1]), groups[2][1])
    es = [jnp.exp2(m - mx) for _, m, _ in groups]
    inv = 1.0 / (es[0] * groups[0][2] + es[1] * groups[1][2] + es[2] * groups[2][2])
    a_out = (spread(es[0] * inv) * groups[0][0] + spread(es[1] * inv) * groups[1][0]
             + spread(es[2] * inv) * groups[2][0])
    a_gate = g_ref[:, G_A_GATE:G_A_GATE + WIDTH_A]
    b_gate = g_ref[:, G_B_GATE:G_B_GATE + WIDTH_B]
    a_in = (a_out * (a_gate * _sigmoid(a_gate)).astype(F32)).astype(BF16)
    b_in = bo_ref[...] * (b_gate * _sigmoid(b_gate))
    a_y = jnp.dot(a_in, wa_ref[...], preferred_element_type=F32)
    b_y = jnp.dot(b_in, wb_ref[...], preferred_element_type=F32)
    merged = (_sigmoid(g_ref[:, G_MG_A:G_MG_A + D_MODEL]).astype(F32) * a_y
              + _sigmoid(g_ref[:, G_MG_B:G_MG_B + D_MODEL]).astype(F32) * b_y)
    x1 = x_ref[...] + jnp.dot(merged.astype(BF16), wo_ref[...], preferred_element_type=F32)
    ms = jnp.mean(x1 * x1, axis=-1, keepdims=True)
    xn = (x1 * lax.rsqrt(ms + EPS) * np_ref[...]).astype(BF16)
    gate = _sigmoid(jnp.dot(xn, wg_ref[...], preferred_element_type=F32))
    ple = jnp.dot(p_ref[...].astype(BF16), wp_ref[...], preferred_element_type=F32)
    y_ref[...] = x1 + gate * ple


def _merge(x3, p4, layer, outs, bo, g, wa, wb, wo, wg, wp, npl, ex, tile):
    b, s, _ = x3.shape
    row = lambda bi, i: (bi, i, 0)
    const = lambda bi, i: (0, 0)

    def rows(width):
        return pl.BlockSpec((None, tile, width), row)

    def grouped(width, d):
        return pl.BlockSpec((None, d, tile // d, width), lambda bi, i: (bi, 0, i, 0))

    def whole(r, c):
        return pl.BlockSpec((r, c), const, pipeline_mode=pl.Buffered(1))

    dils = [d for _, d in DIL_PAIRS]
    return pl.pallas_call(
        _merge_kernel,
        grid=(b, s // tile),
        in_specs=[rows(D_MODEL), pl.BlockSpec((None, None, tile, PLE_DIM), lambda bi, i: (layer, bi, i, 0))]
        + [grouped(WIDTH_A, d) for d in dils] + 2 * [grouped(LANES, d) for d in dils]
        + [rows(WIDTH_B), rows(G_COLS),
           whole(WIDTH_A, D_MODEL), whole(WIDTH_B, D_MODEL), whole(D_MODEL, D_MODEL),
           whole(D_MODEL, D_MODEL), whole(PLE_DIM, D_MODEL), whole(1, D_MODEL),
           whole(LANES, WIDTH_A)],
        out_specs=rows(D_MODEL),
        out_shape=jax.ShapeDtypeStruct((b, s, D_MODEL), F32),
        scratch_shapes=[
            pltpu.VMEM((N_GROUPS_A, WIDTH_A // LANES, tile, LANES), F32),
            pltpu.VMEM((N_GROUPS_A, tile, LANES), F32),
            pltpu.VMEM((N_GROUPS_A, tile, LANES), F32),
        ],
        compiler_params=pltpu.CompilerParams(
            dimension_semantics=("arbitrary", "arbitrary"), vmem_limit_bytes=VMEM_LIMIT),
        name="merge_out",
    )(x3, p4, *[o for o, _, _ in outs], *[m for _, m, _ in outs], *[l for _, _, l in outs],
      bo, g, wa, wb, wo, wg, wp, npl, ex)


def _rope_tables(s):
    inv = ROPE_THETA ** (-jnp.arange(0, ROT_DIM, 2, dtype=F32) / ROT_DIM)
    ang = jnp.arange(s, dtype=F32)[:, None] * inv[None, :]
    ang = jnp.concatenate([ang, ang], axis=-1)
    cos = jnp.cos(ang)
    sin = jnp.sin(ang)
    half = ROT_DIM // 2
    pad = HEAD_DIM - ROT_DIM
    c_head = jnp.concatenate([cos, jnp.ones((s, pad), F32)], axis=1)
    s1_head = jnp.concatenate([-sin[:, :half], jnp.zeros((s, HEAD_DIM - half), F32)], axis=1)
    s2_head = jnp.concatenate([jnp.zeros((s, half), F32), sin[:, half:], jnp.zeros((s, pad), F32)], axis=1)
    reps = LANES // HEAD_DIM
    return jnp.tile(c_head, (1, reps)), jnp.tile(s1_head, (1, reps)), jnp.tile(s2_head, (1, reps))


def _layer_constants():
    lane = jnp.arange(SLAB)
    ones = ((lane[:, None] // HEAD_DIM == lane[None, :] // HEAD_DIM) * (1.0 / HEAD_DIM)).astype(BF16)
    spread = (jnp.arange(LANES)[:, None] == jnp.arange(WIDTH_A)[None, :] // HEAD_DIM).astype(BF16)
    return ones, spread


def _trunk(x, p, norm_mix, w_in, a_q_norm, a_k_norm, b_q_norm, b_k_norm, b_sink, w_branch_a,
           w_branch_b, w_out, norm_ple, w_ple, w_ple_gate):
    b, s, _ = x.shape
    depth = w_in.shape[0]
    row_tile = min(ROW_TILE, s)
    c_t, s1_t, s2_t = _rope_tables(s)
    ones, spread = _layer_constants()
    scale = HEAD_DIM ** -0.5 * LOG2E
    heads = WIDTH_A // HEAD_DIM
    for i in range(depth):
        ga = jnp.concatenate([jnp.tile(a_q_norm[i] * scale, (1, heads)).reshape(1, -1),
                              jnp.tile(a_k_norm[i], (1, heads)).reshape(1, -1)], axis=1)
        gb = jnp.concatenate([jnp.tile(b_q_norm[i] * scale, N_HEADS_B),
                              jnp.tile(b_k_norm[i], KV_WIDTH_B // HEAD_DIM)])[None, :]
        qa0, qa1, qa2, qb, g = _project(x, norm_mix[i][None, :], w_in[i].astype(BF16), ga, gb,
                                        c_t, s1_t, s2_t, ones, row_tile)
        outs = [_attention_a(qkv, window, ATTN_ROWS_A)
                for qkv, (window, _) in zip((qa0, qa1, qa2), DIL_PAIRS)]
        bo = _attention_b(qb, b_sink[i], ATTN_ROWS_B)
        x = _merge(x, p, i, outs, bo, g,
                   w_branch_a[i].astype(BF16), w_branch_b[i].astype(BF16), w_out[i].astype(BF16),
                   w_ple_gate[i].astype(BF16), w_ple[i].astype(BF16), norm_ple[i][None, :],
                   spread, row_tile)
    return x


def kernel(x_prompt, x_sample, p_prompt, p_sample, norm_mix, w_in, a_q_norm, a_k_norm, b_q_norm,
           b_k_norm, b_sink, w_branch_a, w_branch_b, w_out, norm_ple, w_ple, w_ple_gate):
    weights = (norm_mix, w_in, a_q_norm, a_k_norm, b_q_norm, b_k_norm, b_sink, w_branch_a,
               w_branch_b, w_out, norm_ple, w_ple, w_ple_gate)
    return (_trunk(x_prompt, p_prompt, *weights), _trunk(x_sample, p_sample, *weights))
```

```python
import functools

import jax
import jax.numpy as jnp
from jax import lax
from jax.experimental import pallas as pl
from jax.experimental.pallas import tpu as pltpu

D_MODEL = 1024
HEAD_DIM = 64
ROT_DIM = HEAD_DIM // 4
ROPE_THETA = 500000.0
DIL_PAIRS = ((128, 1), (512, 4), (2048, 16))
N_GROUPS_A = 3
WIDTH_A = 512
WIDTH_B = 512
KV_WIDTH_B = 128
N_HEADS_B = 8
WINDOW_B = 128
PLE_DIM = 256
A_QKV_COLS = 3 * N_GROUPS_A * WIDTH_A
IN_COLS = A_QKV_COLS + WIDTH_A + WIDTH_B + 2 * KV_WIDTH_B + WIDTH_B + 2 * D_MODEL
EPS = 1e-6
NEG = -1e30
LOG2E = 1.4426950408889634

OFF_A_GATE = A_QKV_COLS
OFF_BQ = OFF_A_GATE + WIDTH_A
OFF_BK = OFF_BQ + WIDTH_B
OFF_BV = OFF_BK + KV_WIDTH_B
OFF_B_GATE = OFF_BV + KV_WIDTH_B
OFF_MG_A = OFF_B_GATE + WIDTH_B
OFF_MG_B = OFF_MG_A + D_MODEL

G_A_GATE = 0
G_B_GATE = G_A_GATE + WIDTH_A
G_MG_A = G_B_GATE + WIDTH_B
G_MG_B = G_MG_A + D_MODEL
G_COLS = G_MG_B + D_MODEL

LANES = 128
SLAB = 256
HEADS_PER_SLAB = SLAB // HEAD_DIM
Q_BLOCK = 128
ROW_TILE = 512
ATTN_ROWS_A = 4096
ATTN_ROWS_B = 2048
VMEM_LIMIT = 56 * 1024 * 1024

BF16 = jnp.bfloat16
F32 = jnp.float32


def _residue_major(ref, dilation):
    rows = ref.shape[0] // dilation
    return jnp.concatenate([ref[pl.ds(r, rows, stride=dilation), :] for r in range(dilation)], axis=0)


def _proj_kernel(x_ref, nm_ref, w_ref, ga_ref, gb_ref, c_ref, s1_ref, s2_ref, ones_ref,
                 qa0_ref, qa1_ref, qa2_ref, qb_ref, g_ref, hslab):
    xf = x_ref[...]
    ms = jnp.mean(xf * xf, axis=-1, keepdims=True)
    hn32 = xf * lax.rsqrt(ms + EPS) * nm_ref[...]
    n_slabs = D_MODEL // LANES
    for j in range(n_slabs):
        hslab[j] = hn32[:, j * LANES:(j + 1) * LANES]
    ones = ones_ref[...]

    def hn_for(dilation):
        if dilation == 1:
            return hn32.astype(BF16)
        return jnp.concatenate([_residue_major(hslab.at[j], dilation) for j in range(n_slabs)],
                               axis=1).astype(BF16)

    def tables_for(dilation):
        if dilation == 1:
            return c_ref[...], s1_ref[...], s2_ref[...]
        return tuple(_residue_major(t, dilation) for t in (c_ref, s1_ref, s2_ref))

    def wide(t):
        return jnp.concatenate([t, t], axis=1)

    def norm_rope(t, gain, cc, sa, sb, ones_m):
        width = t.shape[1]
        ms = jnp.dot((t * t).astype(BF16), ones_m, preferred_element_type=F32)
        n = t * lax.rsqrt(ms + EPS) * gain
        up = pltpu.roll(n, width - ROT_DIM // 2, 1)
        dn = pltpu.roll(n, ROT_DIM // 2, 1)
        return n * cc + up * sa + dn * sb

    def proj(hn, lo, width):
        return jnp.dot(hn, w_ref[:, lo:lo + width], preferred_element_type=F32)

    def normed(hn, lo, gains, glo, tabs):
        t = proj(hn, lo, WIDTH_A)
        cc, sa, sb = (wide(t_) for t_ in tabs)
        halves = [norm_rope(t[:, h * SLAB:(h + 1) * SLAB], gains[:, glo + h * SLAB:glo + (h + 1) * SLAB],
                            cc, sa, sb, ones) for h in range(WIDTH_A // SLAB)]
        return jnp.concatenate(halves, axis=1).astype(BF16)

    for gi, (out_ref, (_, dilation)) in enumerate(zip((qa0_ref, qa1_ref, qa2_ref), DIL_PAIRS)):
        hn = hn_for(dilation)
        tabs = tables_for(dilation)
        q = normed(hn, gi * WIDTH_A, ga_ref, gi * WIDTH_A, tabs)
        k = normed(hn, (N_GROUPS_A + gi) * WIDTH_A, ga_ref, (N_GROUPS_A + gi) * WIDTH_A, tabs)
        v = proj(hn, (2 * N_GROUPS_A + gi) * WIDTH_A, WIDTH_A).astype(BF16)
        rows = out_ref.shape[1]
        for r in range(dilation):
            rs = slice(r * rows, (r + 1) * rows)
            out_ref[r, :, 0:WIDTH_A] = q[rs]
            out_ref[r, :, WIDTH_A:2 * WIDTH_A] = k[rs]
            out_ref[r, :, 2 * WIDTH_A:] = v[rs]

    hn = hn_for(1)
    tabs = tables_for(1)
    qb_ref[:, 0:WIDTH_B] = normed(hn, OFF_BQ, gb_ref, 0, tabs)
    t = proj(hn, OFF_BK, 2 * KV_WIDTH_B)
    kb = norm_rope(t[:, :KV_WIDTH_B], gb_ref[:, WIDTH_B:WIDTH_B + KV_WIDTH_B], tabs[0], tabs[1], tabs[2],
                   ones[:KV_WIDTH_B, :KV_WIDTH_B])
    qb_ref[:, WIDTH_B:WIDTH_B + KV_WIDTH_B] = kb.astype(BF16)
    qb_ref[:, WIDTH_B + KV_WIDTH_B:] = t[:, KV_WIDTH_B:].astype(BF16)
    g_ref[:, G_A_GATE:G_A_GATE + WIDTH_A] = proj(hn, OFF_A_GATE, WIDTH_A).astype(BF16)
    g_ref[:, G_B_GATE:G_B_GATE + WIDTH_B] = proj(hn, OFF_B_GATE, WIDTH_B).astype(BF16)
    for j in range(2 * D_MODEL // WIDTH_A):
        g_ref[:, G_MG_A + j * WIDTH_A:G_MG_A + (j + 1) * WIDTH_A] = proj(
            hn, OFF_MG_A + j * WIDTH_A, WIDTH_A).astype(BF16)


def _project(x3, nm, w, ga, gb, c_t, s1_t, s2_t, ones, tile):
    b, s, _ = x3.shape
    const = lambda bi, i: (0, 0)
    tab = lambda bi, i: (i, 0)
    row = lambda bi, i: (bi, i, 0)
    qa_specs, qa_shapes = [], []
    for _, d in DIL_PAIRS:
        qa_specs.append(pl.BlockSpec((None, d, tile // d, 3 * WIDTH_A), lambda bi, i: (bi, 0, i, 0)))
        qa_shapes.append(jax.ShapeDtypeStruct((b, d, s // d, 3 * WIDTH_A), BF16))
    return pl.pallas_call(
        _proj_kernel,
        grid=(b, s // tile),
        in_specs=[
            pl.BlockSpec((None, tile, D_MODEL), row),
            pl.BlockSpec((1, D_MODEL), const),
            pl.BlockSpec((D_MODEL, IN_COLS), const, pipeline_mode=pl.Buffered(1)),
            pl.BlockSpec((1, 2 * N_GROUPS_A * WIDTH_A), const),
            pl.BlockSpec((1, WIDTH_B + KV_WIDTH_B), const),
            pl.BlockSpec((tile, LANES), tab),
            pl.BlockSpec((tile, LANES), tab),
            pl.BlockSpec((tile, LANES), tab),
            pl.BlockSpec((SLAB, SLAB), const),
        ],
        out_specs=qa_specs + [
            pl.BlockSpec((None, tile, WIDTH_B + 2 * KV_WIDTH_B), row),
            pl.BlockSpec((None, tile, G_COLS), row),
        ],
        out_shape=qa_shapes + [
            jax.ShapeDtypeStruct((b, s, WIDTH_B + 2 * KV_WIDTH_B), BF16),
            jax.ShapeDtypeStruct((b, s, G_COLS), BF16),
        ],
        scratch_shapes=[pltpu.VMEM((D_MODEL // LANES, tile, LANES), F32)],
        compiler_params=pltpu.CompilerParams(
            dimension_semantics=("arbitrary", "arbitrary"), vmem_limit_bytes=VMEM_LIMIT),
        name="proj_in",
    )(x3, nm, w, ga, gb, c_t, s1_t, s2_t, ones)


def _band_bias(base, half_window, n_keys, length):
    qi = lax.broadcasted_iota(jnp.int32, (Q_BLOCK, 1), 0)
    jj = lax.broadcasted_iota(jnp.int32, (Q_BLOCK, n_keys), 1)
    lo, hi = qi, qi + 2 * half_window
    if base is not None:
        lo = jnp.maximum(lo, half_window - base)
        hi = jnp.minimum(hi, length - 1 - base + half_window)
    return jnp.where(jj >= lo, jnp.where(jj <= hi, 0.0, NEG), NEG).astype(F32)


def _block_biases(t0, n_blocks, half_window, n_keys, length):
    inner = _band_bias(None, half_window, n_keys, length) if n_blocks > 2 else None
    return [_band_bias(t0 + i * Q_BLOCK, half_window, n_keys, length) if i in (0, n_blocks - 1) else inner
            for i in range(n_blocks)]


def _window(prev_ref, main_ref, next_ref, start, n, cols):
    w = prev_ref.shape[0]
    tile = main_ref.shape[0]
    pieces = []
    if start < 0:
        pieces.append(prev_ref[w + start:w, cols])
    pieces.append(main_ref[max(start, 0):min(start + n, tile), cols])
    if start + n > tile:
        pieces.append(next_ref[0:start + n - tile, cols])
    return pieces[0] if len(pieces) == 1 else jnp.concatenate(pieces, axis=0)


def _slab_attention(q_slab, kw, vw, bias, sinks, normalize):
    head_of_lane = lax.broadcasted_iota(jnp.int32, (1, SLAB), 1) // HEAD_DIM
    heads = range(HEADS_PER_SLAB)
    q_bits = pltpu.bitcast(q_slab, jnp.uint32)
    keep = [jnp.where(head_of_lane == h, jnp.uint32(0xFFFFFFFF), jnp.uint32(0)) for h in heads]
    q4 = jnp.concatenate([pltpu.bitcast(q_bits & keep[h], BF16) for h in heads], axis=0)
    s4 = lax.dot_general(q4, kw, (((1,), (1,)), ((), ())), preferred_element_type=F32)
    probs, ms, ls = [], [], []
    for h in heads:
        s = s4[h * Q_BLOCK:(h + 1) * Q_BLOCK] + bias
        m = jnp.max(s, axis=-1, keepdims=True)
        e = jnp.exp2(s - m)
        l = jnp.sum(e, axis=-1, keepdims=True)
        if sinks is not None:
            l = l + jnp.exp2(sinks[h] - m)
        probs.append(e.astype(BF16))
        ms.append(m)
        ls.append(l)
    acc4 = jnp.dot(jnp.concatenate(probs, axis=0), vw, preferred_element_type=F32)
    first_head = lax.broadcasted_iota(jnp.int32, (1, LANES), 1) < HEAD_DIM
    columns = []
    for c in range(SLAB // LANES):
        pair = []
        for h in (2 * c, 2 * c + 1):
            part = acc4[h * Q_BLOCK:(h + 1) * Q_BLOCK, c * LANES:(c + 1) * LANES]
            pair.append(part * (1.0 / ls[h]) if normalize else part)
        columns.append(jnp.where(first_head, pair[0], pair[1]))
    return jnp.concatenate(columns, axis=1), ms, ls


def _attn_a_kernel(q_ref, km_ref, kp_ref, kn_ref, vm_ref, vp_ref, vn_ref, o_ref, m_ref, l_ref,
                   *, half_window, tile, length):
    w = half_window
    n_keys = Q_BLOCK + 2 * w
    n_blocks = tile // Q_BLOCK
    biases = _block_biases(pl.program_id(2) * tile, n_blocks, w, n_keys, length)
    lane = lax.broadcasted_iota(jnp.int32, (1, LANES), 1)
    for rr in range(q_ref.shape[0]):
        k_refs = (kp_ref.at[rr], km_ref.at[rr], kn_ref.at[rr])
        v_refs = (vp_ref.at[rr], vm_ref.at[rr], vn_ref.at[rr])
        for i in range(n_blocks):
            r0 = i * Q_BLOCK
            m_rows = jnp.zeros((Q_BLOCK, LANES), F32)
            l_rows = jnp.ones((Q_BLOCK, LANES), F32)
            for c in range(WIDTH_A // SLAB):
                cols = slice(c * SLAB, (c + 1) * SLAB)
                out, ms, ls = _slab_attention(q_ref[rr, r0:r0 + Q_BLOCK, cols],
                                              _window(*k_refs, r0 - w, n_keys, cols),
                                              _window(*v_refs, r0 - w, n_keys, cols),
                                              biases[i], None, normalize=False)
                o_ref[rr, r0:r0 + Q_BLOCK, cols] = out.astype(BF16)
                for h in range(HEADS_PER_SLAB):
                    m_rows = jnp.where(lane == c * HEADS_PER_SLAB + h, ms[h], m_rows)
                    l_rows = jnp.where(lane == c * HEADS_PER_SLAB + h, ls[h], l_rows)
            m_ref[rr, r0:r0 + Q_BLOCK, :] = m_rows
            l_ref[rr, r0:r0 + Q_BLOCK, :] = l_rows


def _attention_a(qkv, window, rows_per_step):
    b, dilation, length, _ = qkv.shape
    half_window = (window // 2) // dilation
    tile = min(rows_per_step, length)
    residues = min(dilation, rows_per_step // tile)
    halo_per_tile = tile // half_window
    n_halo = length // half_window

    def main(sec):
        return pl.BlockSpec((None, residues, tile, WIDTH_A), lambda bi, r, l: (bi, r, l, sec))

    def prev(sec):
        return pl.BlockSpec((None, residues, half_window, WIDTH_A),
                            lambda bi, r, l: (bi, r, jnp.maximum(l * halo_per_tile - 1, 0), sec))

    def nxt(sec):
        return pl.BlockSpec((None, residues, half_window, WIDTH_A),
                            lambda bi, r, l: (bi, r, jnp.minimum((l + 1) * halo_per_tile, n_halo - 1), sec))

    return pl.pallas_call(
        functools.partial(_attn_a_kernel, half_window=half_window, tile=tile, length=length),
        grid=(b, dilation // residues, length // tile),
        in_specs=[main(0), main(1), prev(1), nxt(1), main(2), prev(2), nxt(2)],
        out_specs=[
            pl.BlockSpec((None, residues, tile, WIDTH_A), lambda bi, r, l: (bi, r, l, 0)),
            pl.BlockSpec((None, residues, tile, LANES), lambda bi, r, l: (bi, r, l, 0)),
            pl.BlockSpec((None, residues, tile, LANES), lambda bi, r, l: (bi, r, l, 0)),
        ],
        out_shape=[
            jax.ShapeDtypeStruct((b, dilation, length, WIDTH_A), BF16),
            jax.ShapeDtypeStruct((b, dilation, length, LANES), F32),
            jax.ShapeDtypeStruct((b, dilation, length, LANES), F32),
        ],
        compiler_params=pltpu.CompilerParams(
            dimension_semantics=("arbitrary", "arbitrary", "arbitrary"),
            vmem_limit_bytes=VMEM_LIMIT),
        name=f"attn_a_d{dilation}",
    )(qkv, qkv, qkv, qkv, qkv, qkv, qkv)


def _attn_b_kernel(sink_ref, q_ref, km_ref, kp_ref, kn_ref, vm_ref, vp_ref, vn_ref, o_ref,
                   *, tile, length):
    w = WINDOW_B
    n_keys = Q_BLOCK + 2 * w
    n_blocks = tile // Q_BLOCK
    biases = _block_biases(pl.program_id(1) * tile, n_blocks, w, n_keys, length)
    even_block = (lax.broadcasted_iota(jnp.int32, (1, SLAB), 1) // HEAD_DIM) % 2 == 0
    all_lanes = slice(0, KV_WIDTH_B)

    def per_kv_head(x):
        swapped = jnp.concatenate([x[:, HEAD_DIM:], x[:, :HEAD_DIM]], axis=1)
        pair = jnp.concatenate([x, x], axis=1)
        riap = jnp.concatenate([swapped, swapped], axis=1)
        return jnp.where(even_block, pair, riap), jnp.where(even_block, riap, pair)

    k_heads = per_kv_head(_window(kp_ref, km_ref, kn_ref, -w, tile + 2 * w, all_lanes))
    v_heads = per_kv_head(_window(vp_ref, vm_ref, vn_ref, -w, tile + 2 * w, all_lanes))
    for i in range(n_blocks):
        r0 = i * Q_BLOCK
        for c in range(WIDTH_B // SLAB):
            sinks = [sink_ref[c * HEADS_PER_SLAB + h] * LOG2E for h in range(HEADS_PER_SLAB)]
            cols = slice(c * SLAB, (c + 1) * SLAB)
            out, _, _ = _slab_attention(q_ref[r0:r0 + Q_BLOCK, cols], k_heads[c][r0:r0 + n_keys],
                                        v_heads[c][r0:r0 + n_keys], biases[i], sinks, normalize=True)
            o_ref[r0:r0 + Q_BLOCK, cols] = out.astype(BF16)


def _attention_b(qb3, sink, tile):
    b, s, _ = qb3.shape
    tile = min(tile, s)
    halo_per_tile = tile // WINDOW_B
    n_halo = s // WINDOW_B
    k_col = WIDTH_B // KV_WIDTH_B
    v_col = k_col + 1

    def main(col):
        return pl.BlockSpec((None, tile, KV_WIDTH_B), lambda bi, l, sk: (bi, l, col))

    def prev(col):
        return pl.BlockSpec((None, WINDOW_B, KV_WIDTH_B),
                            lambda bi, l, sk: (bi, jnp.maximum(l * halo_per_tile - 1, 0), col))

    def nxt(col):
        return pl.BlockSpec((None, WINDOW_B, KV_WIDTH_B),
                            lambda bi, l, sk: (bi, jnp.minimum((l + 1) * halo_per_tile, n_halo - 1), col))

    return pl.pallas_call(
        functools.partial(_attn_b_kernel, tile=tile, length=s),
        grid_spec=pltpu.PrefetchScalarGridSpec(
            num_scalar_prefetch=1,
            grid=(b, s // tile),
            in_specs=[
                pl.BlockSpec((None, tile, WIDTH_B), lambda bi, l, sk: (bi, l, 0)),
                main(k_col), prev(k_col), nxt(k_col), main(v_col), prev(v_col), nxt(v_col),
            ],
            out_specs=pl.BlockSpec((None, tile, WIDTH_B), lambda bi, l, sk: (bi, l, 0)),
        ),
        out_shape=jax.ShapeDtypeStruct((b, s, WIDTH_B), BF16),
        compiler_params=pltpu.CompilerParams(
            dimension_semantics=("arbitrary", "arbitrary"), vmem_limit_bytes=VMEM_LIMIT),
        name="attn_b",
    )(sink, qb3, qb3, qb3, qb3, qb3, qb3, qb3)


def _sigmoid(x):
    half = jnp.asarray(0.5, x.dtype)
    return half * jnp.tanh(half * x) + half


def _merge_kernel(x_ref, p_ref, o0_ref, o1_ref, o2_ref, m0_ref, m1_ref, m2_ref, l0_ref, l1_ref, l2_ref,
                  bo_ref, g_ref, wa_ref, wb_ref, wo_ref, wg_ref, wp_ref, np_ref, ex_ref, y_ref,
                  oslab, mslab, lslab):
    n_slabs = WIDTH_A // LANES

    def token_major(o_ref, m_ref, l_ref, dilation, slot):
        if dilation == 1:
            return o_ref[0].astype(F32), m_ref[0], l_ref[0]
        rows = o_ref.shape[1]
        fine = 4
        for r in range(dilation):
            mslab[slot, pl.ds(r, rows, stride=dilation), :] = m_ref[r]
            lslab[slot, pl.ds(r, rows, stride=dilation), :] = l_ref[r]
        if dilation > fine:
            coarse = dilation // fine
            run = coarse * rows
            for r in range(dilation):
                a, b = r // fine, r % fine
                o_r = o_ref[r].astype(F32)
                for j in range(n_slabs):
                    oslab[0, j, pl.ds(b * run + a, rows, stride=coarse), :] = o_r[:, j * LANES:(j + 1) * LANES]
            for b in range(fine):
                for j in range(n_slabs):
                    oslab[slot, j, pl.ds(b, run, stride=fine), :] = oslab[0, j, b * run:(b + 1) * run, :]
        else:
            for r in range(dilation):
                o_r = o_ref[r].astype(F32)
                for j in range(n_slabs):
                    oslab[slot, j, pl.ds(r, rows, stride=dilation), :] = o_r[:, j * LANES:(j + 1) * LANES]
        return (jnp.concatenate([oslab[slot, j] for j in range(n_slabs)], axis=1), mslab[slot], lslab[slot])

    groups = [token_major(o_ref, m_ref, l_ref, d, slot) for slot, (o_ref, m_ref, l_ref, (_, d)) in
              enumerate(zip((o0_ref, o1_ref, o2_ref), (m0_ref, m1_ref, m2_ref),
                            (l0_ref, l1_ref, l2_ref), DIL_PAIRS))]
    ex = ex_ref[...]

    def spread(wt):
        return jnp.dot(wt.astype(BF16), ex, preferred_element_type=F32)

    mx = jnp.maximum(jnp.maximum(groups[0][1], groups[1][1]), groups[2][1])
    es = [jnp.exp2(m - mx) for _, m, _ in groups]
    inv = 1.0 / (es[0] * groups[0][2] + es[1] * groups[1][2] + es[2] * groups[2][2])
    a_out = (spread(es[0] * inv) * groups[0][0] + spread(es[1] * inv) * groups[1][0]
             + spread(es[2] * inv) * groups[2][0])
    a_gate = g_ref[:, G_A_GATE:G_A_GATE + WIDTH_A]
    b_gate = g_ref[:, G_B_GATE:G_B_GATE + WIDTH_B]
    a_in = (a_out * (a_gate * _sigmoid(a_gate)).astype(F32)).astype(BF16)
    b_in = bo_ref[...] * (b_gate * _sigmoid(b_gate))
    a_y = jnp.dot(a_in, wa_ref[...], preferred_element_type=F32)
    b_y = jnp.dot(b_in, wb_ref[...], preferred_element_type=F32)
    merged = (_sigmoid(g_ref[:, G_MG_A:G_MG_A + D_MODEL]).astype(F32) * a_y
              + _sigmoid(g_ref[:, G_MG_B:G_MG_B + D_MODEL]).astype(F32) * b_y)
    x1 = x_ref[...] + jnp.dot(merged.astype(BF16), wo_ref[...], preferred_element_type=F32)
    ms = jnp.mean(x1 * x1, axis=-1, keepdims=True)
    xn = (x1 * lax.rsqrt(ms + EPS) * np_ref[...]).astype(BF16)
    gate = _sigmoid(jnp.dot(xn, wg_ref[...], preferred_element_type=F32))
    ple = jnp.dot(p_ref[...].astype(BF16), wp_ref[...], preferred_element_type=F32)
    y_ref[...] = x1 + gate * ple


def _merge(x3, p4, layer, outs, bo, g, wa, wb, wo, wg, wp, npl, ex, tile):
    b, s, _ = x3.shape
    row = lambda bi, i: (bi, i, 0)
    const = lambda bi, i: (0, 0)

    def rows(width):
        return pl.BlockSpec((None, tile, width), row)

    def grouped(width, d):
        return pl.BlockSpec((None, d, tile // d, width), lambda bi, i: (bi, 0, i, 0))

    def whole(r, c):
        return pl.BlockSpec((r, c), const, pipeline_mode=pl.Buffered(1))

    dils = [d for _, d in DIL_PAIRS]
    return pl.pallas_call(
        _merge_kernel,
        grid=(b, s // tile),
        in_specs=[rows(D_MODEL), pl.BlockSpec((None, None, tile, PLE_DIM), lambda bi, i: (layer, bi, i, 0))]
        + [grouped(WIDTH_A, d) for d in dils] + 2 * [grouped(LANES, d) for d in dils]
        + [rows(WIDTH_B), rows(G_COLS),
           whole(WIDTH_A, D_MODEL), whole(WIDTH_B, D_MODEL), whole(D_MODEL, D_MODEL),
           whole(D_MODEL, D_MODEL), whole(PLE_DIM, D_MODEL), whole(1, D_MODEL),
           whole(LANES, WIDTH_A)],
        out_specs=rows(D_MODEL),
        out_shape=jax.ShapeDtypeStruct((b, s, D_MODEL), F32),
        scratch_shapes=[
            pltpu.VMEM((N_GROUPS_A, WIDTH_A // LANES, tile, LANES), F32),
            pltpu.VMEM((N_GROUPS_A, tile, LANES), F32),
            pltpu.VMEM((N_GROUPS_A, tile, LANES), F32),
        ],
        compiler_params=pltpu.CompilerParams(
            dimension_semantics=("arbitrary", "arbitrary"), vmem_limit_bytes=VMEM_LIMIT),
        name="merge_out",
    )(x3, p4, *[o for o, _, _ in outs], *[m for _, m, _ in outs], *[l for _, _, l in outs],
      bo, g, wa, wb, wo, wg, wp, npl, ex)


def _rope_tables(s):
    inv = ROPE_THETA ** (-jnp.arange(0, ROT_DIM, 2, dtype=F32) / ROT_DIM)
    ang = jnp.arange(s, dtype=F32)[:, None] * inv[None, :]
    ang = jnp.concatenate([ang, ang], axis=-1)
    cos = jnp.cos(ang)
    sin = jnp.sin(ang)
    half = ROT_DIM // 2
    pad = HEAD_DIM - ROT_DIM
    c_head = jnp.concatenate([cos, jnp.ones((s, pad), F32)], axis=1)
    s1_head = jnp.concatenate([-sin[:, :half], jnp.zeros((s, HEAD_DIM - half), F32)], axis=1)
    s2_head = jnp.concatenate([jnp.zeros((s, half), F32), sin[:, half:], jnp.zeros((s, pad), F32)], axis=1)
    reps = LANES // HEAD_DIM
    return jnp.tile(c_head, (1, reps)), jnp.tile(s1_head, (1, reps)), jnp.tile(s2_head, (1, reps))


def _layer_constants():
    lane = jnp.arange(SLAB)
    ones = ((lane[:, None] // HEAD_DIM == lane[None, :] // HEAD_DIM) * (1.0 / HEAD_DIM)).astype(BF16)
    spread = (jnp.arange(LANES)[:, None] == jnp.arange(WIDTH_A)[None, :] // HEAD_DIM).astype(BF16)
    return ones, spread


def _trunk(x, p, norm_mix, w_in, a_q_norm, a_k_norm, b_q_norm, b_k_norm, b_sink, w_branch_a,
           w_branch_b, w_out, norm_ple, w_ple, w_ple_gate):
    b, s, _ = x.shape
    depth = w_in.shape[0]
    row_tile = min(ROW_TILE, s)
    c_t, s1_t, s2_t = _rope_tables(s)
    ones, spread = _layer_constants()
    scale = HEAD_DIM ** -0.5 * LOG2E
    heads = WIDTH_A // HEAD_DIM
    for i in range(depth):
        ga = jnp.concatenate([jnp.tile(a_q_norm[i] * scale, (1, heads)).reshape(1, -1),
                              jnp.tile(a_k_norm[i], (1, heads)).reshape(1, -1)], axis=1)
        gb = jnp.concatenate([jnp.tile(b_q_norm[i] * scale, N_HEADS_B),
                              jnp.tile(b_k_norm[i], KV_WIDTH_B // HEAD_DIM)])[None, :]
        qa0, qa1, qa2, qb, g = _project(x, norm_mix[i][None, :], w_in[i].astype(BF16), ga, gb,
                                        c_t, s1_t, s2_t, ones, row_tile)
        outs = [_attention_a(qkv, window, ATTN_ROWS_A)
                for qkv, (window, _) in zip((qa0, qa1, qa2), DIL_PAIRS)]
        bo = _attention_b(qb, b_sink[i], ATTN_ROWS_B)
        x = _merge(x, p, i, outs, bo, g,
                   w_branch_a[i].astype(BF16), w_branch_b[i].astype(BF16), w_out[i].astype(BF16),
                   w_ple_gate[i].astype(BF16), w_ple[i].astype(BF16), norm_ple[i][None, :],
                   spread, row_tile)
    return x


def kernel(x_prompt, x_sample, p_prompt, p_sample, norm_mix, w_in, a_q_norm, a_k_norm, b_q_norm,
           b_k_norm, b_sink, w_branch_a, w_branch_b, w_out, norm_ple, w_ple, w_ple_gate):
    weights = (norm_mix, w_in, a_q_norm, a_k_norm, b_q_norm, b_k_norm, b_sink, w_branch_a,
               w_branch_b, w_out, norm_ple, w_ple, w_ple_gate)
    return (_trunk(x_prompt, p_prompt, *weights), _trunk(x_sample, p_sample, *weights))
```
